```python
import math
import jax, jax.numpy as jnp
from jax import lax
import numpy as np

D_MODEL = 1024
BATCH = 16
SEQ = 4096
DEPTH = 4

GRID_W = 64
HEAD_DIM = 64
RWKV_WIDTH = D_MODEL // 2
RWKV_HEADS = RWKV_WIDTH // HEAD_DIM
NA_WIDTH = D_MODEL // 2
NA_HEADS = NA_WIDTH // HEAD_DIM
NA_KH = 8
NA_KW = 16
DECAY_LORA = max(32, int(round(1.8 * RWKV_WIDTH ** 0.5 / 32)) * 32)
ICLR_LORA = max(32, int(round(1.8 * RWKV_WIDTH ** 0.5 / 32)) * 32)
GATE_LORA = max(32, int(round(0.6 * RWKV_WIDTH ** 0.8 / 32)) * 32)
D_FF = -(-8 * D_MODEL // (3 * 256)) * 256
DEEPNORM_ALPHA = (2.0 * DEPTH) ** 0.25
DEEPNORM_BETA = (8.0 * DEPTH) ** -0.25
LN_EPS = 1e-5
GN_EPS = 64e-5
RWKV_COLS = 3 * RWKV_WIDTH + 2 * DECAY_LORA + 2 * ICLR_LORA + GATE_LORA
NA_COLS = 3 * NA_WIDTH
GATE_COLS = 2 * D_MODEL
IN_COLS = RWKV_COLS + NA_COLS + GATE_COLS

kernel_name = "hybrid_rwkv7_natten_deepnorm_encoder"


def _split(t, sizes):
    offs = np.cumsum(np.array(sizes))[:-1].tolist()
    return jnp.split(t, offs, axis=-1)


def _layer_norm(x, g, b):
    x32 = x.astype(jnp.float32)
    mean = jnp.mean(x32, axis=-1, keepdims=True)
    var = jnp.mean(jnp.square(x32 - mean), axis=-1, keepdims=True)
    y = (x32 - mean) * lax.rsqrt(var + LN_EPS) * g + b
    return y.astype(x.dtype)


def _token_shift_centred(u, mu):
    zeros = jnp.zeros_like(u[:, :1])
    u_prev = jnp.concatenate([zeros, u[:, :-1]], axis=1)
    u_next = jnp.concatenate([u[:, 1:], zeros], axis=1)
    return u + mu[0] * (u_prev - u) + mu[1] * (u_next - u)


def _rwkv7_scan(r, w, k, v, a, b, reverse):
    bsz, _, h, n = r.shape
    xs = (jnp.moveaxis(r.astype(jnp.float32), 1, 0),
          jnp.moveaxis(w.astype(jnp.float32), 1, 0),
          jnp.moveaxis(k.astype(jnp.float32), 1, 0),
          jnp.moveaxis(v.astype(jnp.float32), 1, 0),
          jnp.moveaxis(a.astype(jnp.float32), 1, 0),
          jnp.moveaxis(b.astype(jnp.float32), 1, 0))

    def step(state, inp):
        r_t, w_t, k_t, v_t, a_t, b_t = inp
        sa = jnp.einsum('bhvk,bhk->bhv', state, a_t)
        state = (state * w_t[:, :, None, :]
                 + sa[..., None] * b_t[:, :, None, :]
                 + v_t[..., None] * k_t[:, :, None, :])
        out = jnp.einsum('bhvk,bhk->bhv', state, r_t)
        return state, out

    s0 = jnp.zeros((bsz, h, n, n), jnp.float32)
    _, out = lax.scan(step, s0, xs, reverse=reverse)
    return jnp.moveaxis(out, 0, 1)


def _rwkv7_mix(u, mu, decay_w0, decay_up, iclr_a0, iclr_up, gate_up,
               k_k, k_a, r_k, gn_g, gn_b):
    bsz, s, _ = u.shape
    h, n = RWKV_HEADS, HEAD_DIM
    u = _token_shift_centred(u, mu)
    r, k, v, wdn, adn, gdn = _split(
        u, [RWKV_WIDTH, RWKV_WIDTH, RWKV_WIDTH, 2 * DECAY_LORA, 2 * ICLR_LORA, GATE_LORA])
    wdn = wdn.reshape(bsz, s, 2, DECAY_LORA)
    adn = adn.reshape(bsz, s, 2, ICLR_LORA)
    z = decay_w0 + jnp.einsum('bsdr,drc->bsdc', jnp.tanh(wdn), decay_up)
    w_log = -jax.nn.softplus(-z.astype(jnp.float32)) - 0.5
    decay = jnp.exp(-jnp.exp(w_log))
    iclr = jax.nn.sigmoid(iclr_a0 + jnp.einsum('bsdr,drc->bsdc', adn, iclr_up))
    g = jax.nn.sigmoid(gdn) @ gate_up
    kk = (k * k_k).reshape(bsz, s, h, n).astype(jnp.float32)
    kk = kk / jnp.maximum(jnp.linalg.norm(kk, axis=-1, keepdims=True), 1e-12)
    kd = k[:, :, None, :] * (1.0 + (iclr - 1.0) * k_a)
    rh = r.reshape(bsz, s, h, n)
    vh = v.reshape(bsz, s, h, n)
    kd_h = kd.reshape(bsz, s, 2, h, n)
    iclr_h = iclr.reshape(bsz, s, 2, h, n)
    dec_h = decay.reshape(bsz, s, 2, h, n)
    o_fwd = _rwkv7_scan(rh, dec_h[:, :, 0], kd_h[:, :, 0], vh, -kk,
                        kk * iclr_h[:, :, 0], reverse=False)
    o_bwd = _rwkv7_scan(rh, dec_h[:, :, 1], kd_h[:, :, 1], vh, -kk,
                        kk * iclr_h[:, :, 1], reverse=True)
    o = o_fwd + o_bwd
    mean = jnp.mean(o, axis=-1, keepdims=True)
    var = jnp.mean(jnp.square(o - mean), axis=-1, keepdims=True)
    o = ((o - mean) * lax.rsqrt(var + GN_EPS)).reshape(bsz, s, RWKV_WIDTH) * gn_g + gn_b
    kd_sum = (kd_h[:, :, 0] + kd_h[:, :, 1]).astype(jnp.float32)
    bonus = jnp.sum(rh.astype(jnp.float32) * kd_sum * r_k, axis=-1, keepdims=True) * vh
    o = (o + bonus.reshape(bsz, s, RWKV_WIDTH)) * g
    return o.astype(u.dtype)


def _neighbourhood_attn(q, k, v, rpb):
    bsz, s, _ = q.shape
    rows = s // GRID_W
    kh = min(NA_KH, rows)
    kw = NA_KW

    def to_grid(t):
        return t.reshape(bsz, rows, GRID_W, NA_HEADS, HEAD_DIM).transpose(0, 3, 1, 2, 4)

    qg, kg, vg = to_grid(q), to_grid(k), to_grid(v)
    q_rows = jnp.moveaxis(qg, 2, 0) * (HEAD_DIM ** -0.5)
    col = jnp.arange(GRID_W)
    col_start = jnp.clip(col - kw // 2, 0, GRID_W - kw)
    col_idx = col_start[:, None] + jnp.arange(kw)
    dc_idx = col_idx - col[:, None] + (NA_KW - 1)

    def one_row(args):
        q_i, i = args
        rs = jnp.clip(i - kh // 2, 0, rows - kh)
        k_band = lax.dynamic_slice_in_dim(kg, rs, kh, axis=2)[:, :, :, col_idx, :]
        v_band = lax.dynamic_slice_in_dim(vg, rs, kh, axis=2)[:, :, :, col_idx, :]
        dr_idx = rs + jnp.arange(kh) - i + (NA_KH - 1)
        bias = rpb[:, dr_idx[None, :, None], dc_idx[:, None, :]]
        sc = jnp.einsum('bhqn,bhrqcn->bhqrc', q_i, k_band).astype(jnp.float32) + bias
        p = jax.nn.softmax(sc.reshape(bsz, NA_HEADS, GRID_W, kh * kw), axis=-1)
        p = p.reshape(sc.shape).astype(v.dtype)
        return jnp.einsum('bhqrc,bhrqcn->bhqn', p, v_band)

    out = lax.map(one_row, (q_rows, jnp.arange(rows)))
    return out.transpose(1, 0, 3, 2, 4).reshape(bsz, s, NA_WIDTH)


def setup_inputs(seed: int = 0) -> dict:
    key = jax.random.key(seed)
    ks = jax.random.split(key, 32)
    L, D, C = DEPTH, D_MODEL, RWKV_WIDTH
    nrm = jax.random.normal
    f32 = jnp.float32
    return {
        "x": nrm(ks[0], (BATCH, SEQ, D), f32),
        "ln_in_g": 1.0 + 0.02 * nrm(ks[1], (D,), f32),
        "ln_in_b": 0.02 * nrm(ks[2], (D,), f32),
        "w_in": nrm(ks[3], (L, D, IN_COLS), f32) * D ** -0.5,
        "shift_mu": jax.random.uniform(ks[4], (L, 2, RWKV_COLS), f32, 0.0, 0.5),
        "decay_w0": 0.5 * nrm(ks[5], (L, 2, C), f32),
        "decay_up": nrm(ks[6], (L, 2, DECAY_LORA, C), f32) * DECAY_LORA ** -0.5,
        "iclr_a0": 0.5 * nrm(ks[7], (L, 2, C), f32),
        "iclr_up": nrm(ks[8], (L, 2, ICLR_LORA, C), f32) * ICLR_LORA ** -0.5,
        "gate_up": nrm(ks[9], (L, GATE_LORA, C), f32) * GATE_LORA ** -0.5,
        "k_k": 0.85 + 0.05 * nrm(ks[10], (L, C), f32),
        "k_a": 1.0 + 0.05 * nrm(ks[11], (L, C), f32),
        "r_k": 0.1 * nrm(ks[12], (L, RWKV_HEADS, HEAD_DIM), f32),
        "gn_g": 1.0 + 0.02 * nrm(ks[13], (L, C), f32),
        "gn_b": 0.02 * nrm(ks[14], (L, C), f32),
        "na_rpb": 0.1 * nrm(ks[15], (L, NA_HEADS, 2 * NA_KH - 1, 2 * NA_KW - 1), f32),
        "w_branch_rwkv": nrm(ks[16], (L, C, D), f32) * C ** -0.5,
        "w_branch_na": nrm(ks[17], (L, NA_WIDTH, D), f32) * NA_WIDTH ** -0.5,
        "w_out": nrm(ks[18], (L, D, D), f32) * (D ** -0.5 * DEEPNORM_BETA),
        "ln1_g": 1.0 + 0.02 * nrm(ks[19], (L, D), f32),
        "ln1_b": 0.02 * nrm(ks[20], (L, D), f32),
        "w_ffn_in": nrm(ks[21], (L, D, 2 * D_FF), f32) * D ** -0.5,
        "w_ffn_out": nrm(ks[22], (L, D_FF, D), f32) * (D_FF ** -0.5 * DEEPNORM_BETA),
        "ln2_g": 1.0 + 0.02 * nrm(ks[23], (L, D), f32),
        "ln2_b": 0.02 * nrm(ks[24], (L, D), f32),
    }


def reference(x, ln_in_g, ln_in_b, w_in, shift_mu, decay_w0, decay_up, iclr_a0,
              iclr_up, gate_up, k_k, k_a, r_k, gn_g, gn_b, na_rpb, w_branch_rwkv,
              w_branch_na, w_out, ln1_g, ln1_b, w_ffn_in, w_ffn_out, ln2_g, ln2_b):
    x = _layer_norm(x, ln_in_g, ln_in_b)
    for l in range(DEPTH):
        proj = x @ w_in[l]
        u_rwkv, u_na, gates = _split(proj, [RWKV_COLS, NA_COLS, GATE_COLS])
        y_a = _rwkv7_mix(u_rwkv, shift_mu[l], decay_w0[l], decay_up[l], iclr_a0[l],
                         iclr_up[l], gate_up[l], k_k[l], k_a[l], r_k[l], gn_g[l], gn_b[l])
        q, k, v = _split(u_na, [NA_WIDTH, NA_WIDTH, NA_WIDTH])
        y_b = _neighbourhood_attn(q, k, v, na_rpb[l])
        g_a, g_b = _split(gates, [D_MODEL, D_MODEL])
        merged = (jax.nn.sigmoid(g_a) * (y_a @ w_branch_rwkv[l])
                  + jax.nn.sigmoid(g_b) * (y_b @ w_branch_na[l]))
        x = _layer_norm(DEEPNORM_ALPHA * x + merged @ w_out[l], ln1_g[l], ln1_b[l])
        h_gate, h_up = _split(x @ w_ffn_in[l], [D_FF, D_FF])
        ffn = (jax.nn.silu(h_gate) * h_up) @ w_ffn_out[l]
        x = _layer_norm(DEEPNORM_ALPHA * x + ffn, ln2_g[l], ln2_b[l])
    return x
```

```python
import functools
import math

import jax
import jax.numpy as jnp
import numpy as np
from jax import lax
from jax.experimental import pallas as pl
from jax.experimental.pallas import tpu as pltpu

F32 = jnp.float32
BF16 = jnp.bfloat16

HEAD_DIM = 64
GRID_W = 64
NA_KH = 8
NA_KW = 16
NA_ROWS_PER_STEP = 4
NA_BAND_ROWS = NA_KH + NA_ROWS_PER_STEP - 1
LN_EPS = 1e-5
GN_EPS = 64e-5
NEG_BIG = -1e30
LANES = 128
CHUNK = 64
SCAN_BLOCK = 256
HEADS_PER_GROUP = 4
GROUP_LANES = HEADS_PER_GROUP * HEAD_DIM
VMEM_LIMIT = 56 * 1024 * 1024


def _cparams(sem):
    return pltpu.CompilerParams(dimension_semantics=sem, vmem_limit_bytes=VMEM_LIMIT)


def _dot(a, b):
    return jnp.dot(a, b, preferred_element_type=F32)


def _dot_nt(a, b):
    return lax.dot_general(a, b, (((1,), (1,)), ((), ())), preferred_element_type=F32)


def _dot_tn(a, b):
    return lax.dot_general(a, b, (((0,), (0,)), ((), ())), preferred_element_type=F32)


def _layer_norm_rows(x, g, b):
    mean = jnp.mean(x, axis=-1, keepdims=True)
    xc = x - mean
    var = jnp.mean(xc * xc, axis=-1, keepdims=True)
    return xc * lax.rsqrt(var + LN_EPS) * g + b


def _sigmoid(x):
    return 1.0 / (1.0 + jnp.exp(-x))


def _ln_kernel(x_ref, g_ref, b_ref, o_ref, ob_ref):
    y = _layer_norm_rows(x_ref[...], g_ref[...], b_ref[...])
    o_ref[...] = y
    ob_ref[...] = y.astype(BF16)


def _ln_call(x, g, b, tm):
    m, d = x.shape
    return pl.pallas_call(
        _ln_kernel,
        grid=(m // tm,),
        in_specs=[pl.BlockSpec((tm, d), lambda i: (i, 0)),
                  pl.BlockSpec((1, d), lambda i: (0, 0)),
                  pl.BlockSpec((1, d), lambda i: (0, 0))],
        out_specs=[pl.BlockSpec((tm, d), lambda i: (i, 0)),
                   pl.BlockSpec((tm, d), lambda i: (i, 0))],
        out_shape=[jax.ShapeDtypeStruct((m, d), F32), jax.ShapeDtypeStruct((m, d), BF16)],
        compiler_params=_cparams(("parallel",)),
        name="input_ln",
    )(x, g.reshape(1, d), b.reshape(1, d))


def _proj_kernel(x_ref, wr_ref, wn_ref, wg_ref, u_ref, qkv_ref, gate_ref):
    x = x_ref[...]
    u_ref[...] = _dot(x, wr_ref[...])
    qkv_ref[...] = _dot(x, wn_ref[...]).astype(BF16)
    gate_ref[...] = _dot(x, wg_ref[...]).astype(BF16)


def _proj_call(xb, w_rwkv, w_na, w_gate, tm):
    m, d = xb.shape
    nr, nn, ng = w_rwkv.shape[1], w_na.shape[1], w_gate.shape[1]
    const = lambda i: (0, 0)
    return pl.pallas_call(
        _proj_kernel,
        grid=(m // tm,),
        in_specs=[pl.BlockSpec((tm, d), lambda i: (i, 0)),
                  pl.BlockSpec((d, nr), const),
                  pl.BlockSpec((d, nn), const),
                  pl.BlockSpec((d, ng), const)],
        out_specs=[pl.BlockSpec((tm, nr), lambda i: (i, 0)),
                   pl.BlockSpec((tm, nn), lambda i: (i, 0)),
                   pl.BlockSpec((tm, ng), lambda i: (i, 0))],
        out_shape=[jax.ShapeDtypeStruct((m, nr), F32),
                   jax.ShapeDtypeStruct((m, nn), BF16),
                   jax.ShapeDtypeStruct((m, ng), BF16)],
        compiler_params=_cparams(("parallel",)),
        name="in_proj",
    )(xb, w_rwkv, w_na, w_gate)


def _block_diag(x, bd_mask):
    tiled = jnp.concatenate([x] * HEADS_PER_GROUP, axis=0)
    return jnp.where(bd_mask, tiled, jnp.zeros_like(tiled))


def _rwkv_kernel(u_ref, up_ref, un_ref, mu_ref, wlr_ref, w0_ref, a0_ref, kk_ref, ka_ref,
                 rk_ref, ones_ref, tri_ref, o_ref, bonus_ref, g_ref, state_ref, *,
                 reverse, width):
    t = pl.program_id(1)
    nt = pl.num_programs(1)
    tb = u_ref.shape[1]
    c3 = 3 * width

    @pl.when(t == 0)
    def _():
        state_ref[...] = jnp.zeros_like(state_ref)

    pos = (nt - 1 - t) if reverse else t

    u = u_ref[0]
    row = lax.broadcasted_iota(jnp.int32, u.shape, 0)
    prev_row = jnp.where(pos == 0, 0.0, up_ref[0, 7:8, :])
    next_row = jnp.where(pos == nt - 1, 0.0, un_ref[0, 0:1, :])
    u_prev = jnp.where(row == 0, prev_row, pltpu.roll(u, 1, axis=0))
    u_next = jnp.where(row == tb - 1, next_row, pltpu.roll(u, tb - 1, axis=0))
    us = u + mu_ref[0:1, :] * (u_prev - u) + mu_ref[1:2, :] * (u_next - u)

    r = us[:, 0:width]
    k = us[:, width:2 * width]
    v = us[:, 2 * width:c3]
    low = us[:, c3:c3 + 2 * LANES]

    lane = lax.broadcasted_iota(jnp.int32, low.shape, 1)
    f = jnp.where(lane < 64, jnp.tanh(low), jnp.where(lane < LANES, low, _sigmoid(low)))
    lr = _dot(f.astype(BF16), wlr_ref[...])
    z = w0_ref[...] + lr[:, 0:width]
    iclr = _sigmoid(a0_ref[...] + lr[:, width:2 * width])
    g_ref[0] = lr[:, 2 * width:3 * width]

    logw = -math.exp(-0.5) * _sigmoid(z)

    ones_bd = ones_ref[...]
    kkr = k * kk_ref[...]
    ss = _dot((kkr * kkr).astype(BF16), ones_bd)
    kk = kkr * lax.rsqrt(jnp.maximum(ss, 1e-24))
    kd = k * (1.0 + (iclr - 1.0) * ka_ref[...])
    a = -kk
    b = kk * iclr

    bonus_ref[0] = _dot((r * kd * rk_ref[...]).astype(BF16), ones_bd) * v

    hi = logw.astype(BF16)
    r1 = logw - hi.astype(F32)
    mid = r1.astype(BF16)
    lo = (r1 - mid.astype(F32)).astype(BF16)
    tri = tri_ref[...]
    cum = _dot(tri, hi) + _dot(tri, mid) + _dot(tri, lo)

    e_incl = jnp.exp(cum)
    a_t = (a * jnp.exp(cum - logw)).astype(BF16)
    r_t = (r * e_incl).astype(BF16)
    e_neg = jnp.exp(-cum)
    b_t = (b * e_neg).astype(BF16)
    k_t = (kd * e_neg).astype(BF16)
    v_b = v.astype(BF16)

    shape_c = (CHUNK, GROUP_LANES)
    ti = lax.broadcasted_iota(jnp.int32, shape_c, 0)
    ji = lax.broadcasted_iota(jnp.int32, shape_c, 1) % CHUNK
    strict = (ji > ti) if reverse else (ji < ti)
    incl = (ji >= ti) if reverse else (ji <= ti)
    eye = jnp.where(ji == ti, 1.0, 0.0).astype(F32)
    same16 = (ti // 16) == (ji // 16)
    same32 = (ti // 32) == (ji // 32)
    lvl0 = strict & same16
    lvl1 = strict & same32 & jnp.logical_not(same16)
    lvl2 = strict & jnp.logical_not(same32)
    bi = lax.broadcasted_iota(jnp.int32, (GROUP_LANES, GROUP_LANES), 0) // HEAD_DIM
    bj = lax.broadcasted_iota(jnp.int32, (GROUP_LANES, GROUP_LANES), 1) // HEAD_DIM
    bd_mask = bi == bj

    def mm(x, y):
        return _dot(x.astype(BF16), _block_diag(y.astype(BF16), bd_mask))

    n_chunks = tb // CHUNK
    n_groups = width // GROUP_LANES
    order = range(n_chunks - 1, -1, -1) if reverse else range(n_chunks)

    pre = {}
    for c in range(n_chunks):
        rs = slice(c * CHUNK, (c + 1) * CHUNK)
        for gi in range(n_groups):
            ls = slice(gi * GROUP_LANES, (gi + 1) * GROUP_LANES)
            at_c, rt_c, bt_c, kt_c, v_c = a_t[rs, ls], r_t[rs, ls], b_t[rs, ls], k_t[rs, ls], v_b[rs, ls]
            ar = jnp.concatenate([at_c, rt_c], axis=0)
            ab = _dot_nt(ar, _block_diag(bt_c, bd_mask))
            ak = _dot_nt(ar, _block_diag(kt_c, bd_mask))
            n_all = ab[0:CHUNK]
            a_rb = jnp.where(incl, ab[CHUNK:], 0.0).astype(BF16)
            a_ak = jnp.where(strict, ak[0:CHUNK], 0.0)
            a_rk = jnp.where(incl, ak[CHUNK:], 0.0)
            n0 = jnp.where(lvl0, n_all, 0.0)
            p2 = mm(n0, n0)
            x = eye + n0
            x = x + mm(x, p2)
            p4 = mm(p2, p2)
            x = x + mm(x, p4)
            p8 = mm(p4, p4)
            x = x + mm(x, p8)
            n1 = jnp.where(lvl1, n_all, 0.0)
            x = x + mm(mm(x, n1), x)
            n2 = jnp.where(lvl2, n_all, 0.0)
            x = x + mm(mm(x, n2), x)
            tinv = x.astype(BF16)
            akv = _dot(jnp.concatenate([a_ak, a_rk], axis=0).astype(BF16), _block_diag(v_c, bd_mask))
            w_til = _dot(tinv, _block_diag(at_c, bd_mask)).astype(BF16)
            u_til = _dot(tinv, _block_diag(akv[0:CHUNK].astype(BF16), bd_mask))
            pre[(c, gi)] = (w_til, u_til, a_rb, akv[CHUNK:], rt_c, bt_c, kt_c, v_c)

    for gi in range(n_groups):
        ls = slice(gi * GROUP_LANES, (gi + 1) * GROUP_LANES)
        s = state_ref[gi]
        for c in order:
            w_til, u_til, a_rb, o_intra, rt_c, bt_c, kt_c, v_c = pre[(c, gi)]
            sb = s.astype(BF16)
            uu = _dot_nt(w_til, sb) + u_til
            uub = uu.astype(BF16)
            o = _dot_nt(rt_c, sb) + _dot(a_rb, _block_diag(uub, bd_mask)) + o_intra
            o_ref[0, c * CHUNK:(c + 1) * CHUNK, ls] = o
            upd = _dot_tn(jnp.concatenate([uub, v_c], axis=0), jnp.concatenate([bt_c, kt_c], axis=0))
            edge = c * CHUNK if reverse else (c + 1) * CHUNK - 1
            s = (s + jnp.where(bd_mask, upd, 0.0)) * e_incl[edge:edge + 1, ls]
        state_ref[gi] = s


def _rwkv_call(u3, mu, wlr, w0, a0, k_k, k_a, r_k, ones_bd, tri, *, reverse, width):
    bsz, s, cols = u3.shape
    tb = SCAN_BLOCK
    nt = s // tb
    hb = tb // 8
    last8 = s // 8 - 1

    def blk(t):
        return (nt - 1 - t) if reverse else t

    kern = functools.partial(_rwkv_kernel, reverse=reverse, width=width)
    const2 = lambda b, t: (0, 0)
    out_sds = jax.ShapeDtypeStruct((bsz, s, width), F32)
    out_spec = pl.BlockSpec((1, tb, width), lambda b, t: (b, blk(t), 0))
    return pl.pallas_call(
        kern,
        grid=(bsz, nt),
        in_specs=[pl.BlockSpec((1, tb, cols), lambda b, t: (b, blk(t), 0)),
                  pl.BlockSpec((1, 8, cols), lambda b, t: (b, jnp.maximum(blk(t) * hb - 1, 0), 0)),
                  pl.BlockSpec((1, 8, cols), lambda b, t: (b, jnp.minimum((blk(t) + 1) * hb, last8), 0)),
                  pl.BlockSpec(mu.shape, const2),
                  pl.BlockSpec(wlr.shape, const2),
                  pl.BlockSpec((1, width), const2),
                  pl.BlockSpec((1, width), const2),
                  pl.BlockSpec((1, width), const2),
                  pl.BlockSpec((1, width), const2),
                  pl.BlockSpec((1, width), const2),
                  pl.BlockSpec(ones_bd.shape, const2),
                  pl.BlockSpec(tri.shape, const2)],
        out_specs=[out_spec, out_spec, out_spec],
        out_shape=[out_sds, out_sds, out_sds],
        scratch_shapes=[pltpu.VMEM((width // GROUP_LANES, GROUP_LANES, GROUP_LANES), F32)],
        compiler_params=_cparams(("parallel", "arbitrary")),
        name="rwkv_bwd" if reverse else "rwkv_fwd",
    )(u3, u3, u3, mu, wlr, w0, a0, k_k, k_a, r_k, ones_bd, tri)


def _na_kernel(q_ref, k_ref, v_ref, bias_ref, o_ref, *, rows):
    g = pl.program_id(1)
    band = NA_BAND_ROWS * GRID_W
    start_row = jnp.clip(g * NA_ROWS_PER_STEP - NA_KH // 2, 0, rows - NA_BAND_ROWS)
    start = pl.multiple_of(start_row * GRID_W, GRID_W)
    n_pairs = q_ref.shape[2] // LANES
    lane = lax.broadcasted_iota(jnp.int32, (band, LANES), 1)
    for p in range(n_pairs):
        ls = slice(p * LANES, (p + 1) * LANES)
        qp = q_ref[0, :, ls] * 0.125
        kp = k_ref[0, pl.ds(start, band), ls]
        vp = v_ref[0, pl.ds(start, band), ls]
        acc = None
        for hh in range(2):
            sel = (lane < HEAD_DIM) if hh == 0 else (lane >= HEAD_DIM)
            km = jnp.where(sel, kp, jnp.zeros_like(kp))
            vm = jnp.where(sel, vp, jnp.zeros_like(vp))
            sc = _dot_nt(qp, km) + bias_ref[0, 2 * p + hh]
            m = jnp.max(sc, axis=-1, keepdims=True)
            e = jnp.exp(sc - m)
            inv = 1.0 / jnp.sum(e, axis=-1, keepdims=True)
            part = _dot(e.astype(BF16), vm) * inv
            acc = part if acc is None else acc + part
        o_ref[0, :, ls] = acc.astype(o_ref.dtype)


def _na_call(qkv3, bias, width):
    bsz, s, _ = qkv3.shape
    rows = s // GRID_W
    tq = NA_ROWS_PER_STEP * GRID_W
    ng = rows // NA_ROWS_PER_STEP
    nh = width // HEAD_DIM

    def bias_idx(b, g):
        ty = jnp.where(g == 0, 0, jnp.where(g == ng - 1, 2, 1))
        return (ty, 0, 0, 0)

    return pl.pallas_call(
        functools.partial(_na_kernel, rows=rows),
        grid=(bsz, ng),
        in_specs=[pl.BlockSpec((1, tq, width), lambda b, g: (b, g, 0)),
                  pl.BlockSpec((1, s, width), lambda b, g: (b, 0, 1)),
                  pl.BlockSpec((1, s, width), lambda b, g: (b, 0, 2)),
                  pl.BlockSpec((1, nh, tq, NA_BAND_ROWS * GRID_W), bias_idx)],
        out_specs=pl.BlockSpec((1, tq, width), lambda b, g: (b, g, 0)),
        out_shape=jax.ShapeDtypeStruct((bsz, s, width), BF16),
        compiler_params=_cparams(("parallel", "arbitrary")),
        name="nbr_attn",
    )(qkv3, qkv3, qkv3, bias)


def _na_bias_tables(rpb, rows):
    ng = rows // NA_ROWS_PER_STEP
    kh = min(NA_KH, rows)
    tables = []
    for g in (0, 1, ng - 1):
        bs = int(np.clip(g * NA_ROWS_PER_STEP - NA_KH // 2, 0, rows - NA_BAND_ROWS))
        qi = g * NA_ROWS_PER_STEP + np.arange(NA_ROWS_PER_STEP)
        kr = bs + np.arange(NA_BAND_ROWS)
        rs = np.clip(qi - kh // 2, 0, rows - kh)
        ok_r = (kr[None, :] >= rs[:, None]) & (kr[None, :] < rs[:, None] + kh)
        dr = np.clip(kr[None, :] - qi[:, None] + (NA_KH - 1), 0, 2 * NA_KH - 2)
        col = np.arange(GRID_W)
        cs = np.clip(col - NA_KW // 2, 0, GRID_W - NA_KW)
        ok_c = (col[None, :] >= cs[:, None]) & (col[None, :] < cs[:, None] + NA_KW)
        dc = np.clip(col[None, :] - col[:, None] + (NA_KW - 1), 0, 2 * NA_KW - 2)
        vals = rpb[:, dr[:, None, :, None], dc[None, :, None, :]]
        ok = ok_r[:, None, :, None] & ok_c[None, :, None, :]
        tab = jnp.where(ok[None], vals, NEG_BIG)
        tables.append(tab.reshape(rpb.shape[0], NA_ROWS_PER_STEP * GRID_W, NA_BAND_ROWS * GRID_W))
    return jnp.stack(tables).astype(F32)


def _merge_kernel(o0_ref, o1_ref, b0_ref, b1_ref, g_ref, yb_ref, gate_ref, x_ref, ones_ref,
                  gng_ref, gnb_ref, wa_ref, wb_ref, wo_ref, lg_ref, lb_ref, xo_ref, xob_ref, *,
                  alpha, d_model):
    ones_bd = ones_ref[...]
    inv_n = 1.0 / HEAD_DIM
    o = o0_ref[...] + o1_ref[...]
    mean = _dot(o.astype(BF16), ones_bd) * inv_n
    oc = o - mean
    var = _dot((oc * oc).astype(BF16), ones_bd) * inv_n
    on = oc * lax.rsqrt(var + GN_EPS) * gng_ref[...] + gnb_ref[...]
    ya = (on + b0_ref[...] + b1_ref[...]) * g_ref[...]
    gates = gate_ref[...].astype(F32)
    ga = _sigmoid(gates[:, 0:d_model])
    gb = _sigmoid(gates[:, d_model:])
    merged = ga * _dot(ya.astype(BF16), wa_ref[...]) + gb * _dot(yb_ref[...], wb_ref[...])
    y = alpha * x_ref[...] + _dot(merged.astype(BF16), wo_ref[...])
    xn = _layer_norm_rows(y, lg_ref[...], lb_ref[...])
    xo_ref[...] = xn
    xob_ref[...] = xn.astype(BF16)


def _merge_call(o0, o1, b0, b1, g, yb, gates, x, ones_bd, gn_g, gn_b, wa, wb, wo, ln_g, ln_b,
                alpha, tm):
    m, d = x.shape
    w = o0.shape[1]
    rowblk = lambda n: pl.BlockSpec((tm, n), lambda i: (i, 0))
    const = lambda shape: pl.BlockSpec(shape, lambda i: (0, 0))
    return pl.pallas_call(
        functools.partial(_merge_kernel, alpha=alpha, d_model=d),
        grid=(m // tm,),
        in_specs=[rowblk(w), rowblk(w), rowblk(w), rowblk(w), rowblk(w), rowblk(w),
                  rowblk(2 * d), rowblk(d), const(ones_bd.shape),
                  const((1, w)), const((1, w)), const(wa.shape), const(wb.shape), const(wo.shape),
                  const((1, d)), const((1, d))],
        out_specs=[rowblk(d), rowblk(d)],
        out_shape=[jax.ShapeDtypeStruct((m, d), F32), jax.ShapeDtypeStruct((m, d), BF16)],
        compiler_params=_cparams(("parallel",)),
        name="merge_out_ln",
    )(o0, o1, b0, b1, g, yb, gates, x, ones_bd, gn_g, gn_b, wa, wb, wo, ln_g, ln_b)


def _ffn_kernel(x_ref, xb_ref, wg_ref, wu_ref, wo_ref, lg_ref, lb_ref, xo_ref, xob_ref, *, alpha):
    xb = xb_ref[...]
    hg = _dot(xb, wg_ref[...])
    hu = _dot(xb, wu_ref[...])
    act = (hg * _sigmoid(hg) * hu).astype(BF16)
    y = alpha * x_ref[...] + _dot(act, wo_ref[...])
    xn = _layer_norm_rows(y, lg_ref[...], lb_ref[...])
    xo_ref[...] = xn
    xob_ref[...] = xn.astype(BF16)


def _ffn_call(x, xb, wg, wu, wo, ln_g, ln_b, alpha, tm):
    m, d = x.shape
    rowblk = pl.BlockSpec((tm, d), lambda i: (i, 0))
    const = lambda shape: pl.BlockSpec(shape, lambda i: (0, 0), pipeline_mode=pl.Buffered(1))
    return pl.pallas_call(
        functools.partial(_ffn_kernel, alpha=alpha),
        grid=(m // tm,),
        in_specs=[rowblk, rowblk, const(wg.shape), const(wu.shape), const(wo.shape),
                  pl.BlockSpec((1, d), lambda i: (0, 0)), pl.BlockSpec((1, d), lambda i: (0, 0))],
        out_specs=[rowblk, rowblk],
        out_shape=[jax.ShapeDtypeStruct((m, d), F32), jax.ShapeDtypeStruct((m, d), BF16)],
        compiler_params=_cparams(("parallel",)),
        name="swiglu_ln",
    )(x, xb, wg, wu, wo, ln_g, ln_b)


def _low_rank_weight(decay_up_d, iclr_up_d, gate_up, direction, width):
    r_dec, r_icl, r_gate = decay_up_d.shape[0], iclr_up_d.shape[0], gate_up.shape[0]
    w = jnp.zeros((2 * LANES, 3 * width), F32)
    w = w.at[direction * r_dec:(direction + 1) * r_dec, 0:width].set(decay_up_d)
    w = w.at[64 + direction * r_icl:64 + (direction + 1) * r_icl, width:2 * width].set(iclr_up_d)
    w = w.at[LANES:LANES + r_gate, 2 * width:].set(gate_up)
    return w.astype(BF16)


def kernel(x, ln_in_g, ln_in_b, w_in, shift_mu, decay_w0, decay_up, iclr_a0, iclr_up, gate_up,
           k_k, k_a, r_k, gn_g, gn_b, na_rpb, w_branch_rwkv, w_branch_na, w_out, ln1_g, ln1_b,
           w_ffn_in, w_ffn_out, ln2_g, ln2_b):
    bsz, s, d = x.shape
    depth = w_in.shape[0]
    width = k_k.shape[1]
    na_width = w_branch_na.shape[1]
    d_ff = w_ffn_out.shape[1]
    rwkv_cols = shift_mu.shape[2]
    rwkv_pad = 3 * width + 2 * LANES
    assert 2 * decay_up.shape[2] == 64 and 2 * iclr_up.shape[2] == 64 and gate_up.shape[1] <= LANES
    assert rwkv_cols == 3 * width + LANES + gate_up.shape[1] and s % SCAN_BLOCK == 0
    rows = s // GRID_W
    assert rows % NA_ROWS_PER_STEP == 0 and rows >= NA_BAND_ROWS and rows // NA_ROWS_PER_STEP >= 3
    alpha = (2.0 * depth) ** 0.25
    m = bsz * s
    tm = 512 if m % 512 == 0 else 256

    ones_bd = jnp.asarray(np.kron(np.eye(width // HEAD_DIM), np.ones((HEAD_DIM, HEAD_DIM))), BF16)
    pos = np.arange(SCAN_BLOCK)
    same_chunk = (pos[:, None] // CHUNK) == (pos[None, :] // CHUNK)
    tri_f = jnp.asarray(same_chunk & (pos[None, :] <= pos[:, None]), BF16)
    tri_r = jnp.asarray(same_chunk & (pos[None, :] >= pos[:, None]), BF16)

    x2, xb = _ln_call(x.reshape(m, d), ln_in_g, ln_in_b, tm)
    row = lambda p: p.reshape(1, -1)
    for l in range(depth):
        wl = w_in[l]
        w_r = jnp.pad(wl[:, :rwkv_cols], ((0, 0), (0, rwkv_pad - rwkv_cols))).astype(BF16)
        w_n = wl[:, rwkv_cols:rwkv_cols + 3 * na_width].astype(BF16)
        w_g = wl[:, rwkv_cols + 3 * na_width:].astype(BF16)
        mu = jnp.pad(shift_mu[l], ((0, 0), (0, rwkv_pad - rwkv_cols)))

        u, qkv, gates = _proj_call(xb, w_r, w_n, w_g, tm)
        u3 = u.reshape(bsz, s, rwkv_pad)

        outs = []
        for direction, tri in ((0, tri_f), (1, tri_r)):
            wlr = _low_rank_weight(decay_up[l, direction], iclr_up[l, direction], gate_up[l],
                                   direction, width)
            outs.append(_rwkv_call(
                u3, mu, wlr, row(decay_w0[l, direction]), row(iclr_a0[l, direction]),
                row(k_k[l]), row(k_a[l]), row(r_k[l]), ones_bd, tri,
                reverse=bool(direction), width=width))
        (o0, b0, g0), (o1, b1, _) = outs

        bias = _na_bias_tables(na_rpb[l], rows)
        yb = _na_call(qkv.reshape(bsz, s, 3 * na_width), bias, na_width)

        flat = lambda t: t.reshape(m, -1)
        x2, xb = _merge_call(
            flat(o0), flat(o1), flat(b0), flat(b1), flat(g0), flat(yb), gates, x2, ones_bd,
            row(gn_g[l]), row(gn_b[l]), w_branch_rwkv[l].astype(BF16), w_branch_na[l].astype(BF16),
            w_out[l].astype(BF16), row(ln1_g[l]), row(ln1_b[l]), alpha, tm)

        x2, xb = _ffn_call(
            x2, xb, w_ffn_in[l, :, :d_ff].astype(BF16), w_ffn_in[l, :, d_ff:].astype(BF16),
            w_ffn_out[l].astype(BF16), row(ln2_g[l]), row(ln2_b[l]), alpha, 256)
    return x2.reshape(bsz, s, d)
```

```python
import functools
import math

import jax
import jax.numpy as jnp
import numpy as np
from jax import lax
from jax.experimental import pallas as pl
from jax.experimental.pallas import tpu as pltpu

F32 = jnp.float32
BF16 = jnp.bfloat16

HEAD_DIM = 64
GRID_W = 64
NA_KH = 8
NA_KW = 16
NA_ROWS_PER_STEP = 4
NA_BAND_ROWS = NA_KH + NA_ROWS_PER_STEP - 1
LN_EPS = 1e-5
GN_EPS = 64e-5
NEG_BIG = -1e30
LANES = 128
CHUNK = 64
SCAN_BLOCK = 256
HEADS_PER_GROUP = 4
GROUP_LANES = HEADS_PER_GROUP * HEAD_DIM
VMEM_LIMIT = 56 * 1024 * 1024


def _cparams(sem):
    return pltpu.CompilerParams(dimension_semantics=sem, vmem_limit_bytes=VMEM_LIMIT)


def _dot(a, b):
    return jnp.dot(a, b, preferred_element_type=F32)


def _dot_nt(a, b):
    return lax.dot_general(a, b, (((1,), (1,)), ((), ())), preferred_element_type=F32)


def _dot_tn(a, b):
    return lax.dot_general(a, b, (((0,), (0,)), ((), ())), preferred_element_type=F32)


def _layer_norm_rows(x, g, b):
    mean = jnp.mean(x, axis=-1, keepdims=True)
    xc = x - mean
    var = jnp.mean(xc * xc, axis=-1, keepdims=True)
    return xc * lax.rsqrt(var + LN_EPS) * g + b


def _sigmoid(x):
    return 1.0 / (1.0 + jnp.exp(-x))


def _ln_kernel(x_ref, g_ref, b_ref, o_ref, ob_ref):
    y = _layer_norm_rows(x_ref[...], g_ref[...], b_ref[...])
    o_ref[...] = y
    ob_ref[...] = y.astype(BF16)


def _ln_call(x, g, b, tm):
    m, d = x.shape
    return pl.pallas_call(
        _ln_kernel,
        grid=(m // tm,),
        in_specs=[pl.BlockSpec((tm, d), lambda i: (i, 0)),
                  pl.BlockSpec((1, d), lambda i: (0, 0)),
                  pl.BlockSpec((1, d), lambda i: (0, 0))],
        out_specs=[pl.BlockSpec((tm, d), lambda i: (i, 0)),
                   pl.BlockSpec((tm, d), lambda i: (i, 0))],
        out_shape=[jax.ShapeDtypeStruct((m, d), F32), jax.ShapeDtypeStruct((m, d), BF16)],
        compiler_params=_cparams(("parallel",)),
        name="input_ln",
    )(x, g.reshape(1, d), b.reshape(1, d))


def _proj_kernel(x_ref, wr_ref, wn_ref, wg_ref, u_ref, qkv_ref, gate_ref):
    x = x_ref[...]
    u_ref[...] = _dot(x, wr_ref[...])
    qkv_ref[...] = _dot(x, wn_ref[...]).astype(BF16)
    gate_ref[...] = _dot(x, wg_ref[...]).astype(BF16)


def _proj_call(xb, w_rwkv, w_na, w_gate, tm):
    m, d = xb.shape
    nr, nn, ng = w_rwkv.shape[1], w_na.shape[1], w_gate.shape[1]
    const = lambda i: (0, 0)
    return pl.pallas_call(
        _proj_kernel,
        grid=(m // tm,),
        in_specs=[pl.BlockSpec((tm, d), lambda i: (i, 0)),
                  pl.BlockSpec((d, nr), const),
                  pl.BlockSpec((d, nn), const),
                  pl.BlockSpec((d, ng), const)],
        out_specs=[pl.BlockSpec((tm, nr), lambda i: (i, 0)),
                   pl.BlockSpec((tm, nn), lambda i: (i, 0)),
                   pl.BlockSpec((tm, ng), lambda i: (i, 0))],
        out_shape=[jax.ShapeDtypeStruct((m, nr), F32),
                   jax.ShapeDtypeStruct((m, nn), BF16),
                   jax.ShapeDtypeStruct((m, ng), BF16)],
        compiler_params=_cparams(("parallel",)),
        name="in_proj",
    )(xb, w_rwkv, w_na, w_gate)


def _block_diag(x, bd_mask):
    tiled = jnp.concatenate([x] * HEADS_PER_GROUP, axis=0)
    return jnp.where(bd_mask, tiled, jnp.zeros_like(tiled))


def _rwkv_kernel(u_ref, up_ref, un_ref, mu_ref, wlr_ref, w0_ref, a0_ref, kk_ref, ka_ref,
                 rk_ref, ones_ref, tri_ref, o_ref, bonus_ref, g_ref, state_ref, *,
                 reverse, width):
    t = pl.program_id(1)
    nt = pl.num_programs(1)
    tb = u_ref.shape[1]
    c3 = 3 * width

    @pl.when(t == 0)
    def _():
        state_ref[...] = jnp.zeros_like(state_ref)

    pos = (nt - 1 - t) if reverse else t

    u = u_ref[0]
    row = lax.broadcasted_iota(jnp.int32, u.shape, 0)
    prev_row = jnp.where(pos == 0, 0.0, up_ref[0, 7:8, :])
    next_row = jnp.where(pos == nt - 1, 0.0, un_ref[0, 0:1, :])
    u_prev = jnp.where(row == 0, prev_row, pltpu.roll(u, 1, axis=0))
    u_next = jnp.where(row == tb - 1, next_row, pltpu.roll(u, tb - 1, axis=0))
    us = u + mu_ref[0:1, :] * (u_prev - u) + mu_ref[1:2, :] * (u_next - u)

    r = us[:, 0:width]
    k = us[:, width:2 * width]
    v = us[:, 2 * width:c3]
    low = us[:, c3:c3 + 2 * LANES]

    lane = lax.broadcasted_iota(jnp.int32, low.shape, 1)
    f = jnp.where(lane < 64, jnp.tanh(low), jnp.where(lane < LANES, low, _sigmoid(low)))
    lr = _dot(f.astype(BF16), wlr_ref[...])
    z = w0_ref[...] + lr[:, 0:width]
    iclr = _sigmoid(a0_ref[...] + lr[:, width:2 * width])
    g_ref[0] = lr[:, 2 * width:3 * width]

    logw = -math.exp(-0.5) * _sigmoid(z)

    ones_bd = ones_ref[...]
    kkr = k * kk_ref[...]
    ss = _dot((kkr * kkr).astype(BF16), ones_bd)
    kk = kkr * lax.rsqrt(jnp.maximum(ss, 1e-24))
    kd = k * (1.0 + (iclr - 1.0) * ka_ref[...])
    a = -kk
    b = kk * iclr

    bonus_ref[0] = _dot((r * kd * rk_ref[...]).astype(BF16), ones_bd) * v

    hi = logw.astype(BF16)
    r1 = logw - hi.astype(F32)
    mid = r1.astype(BF16)
    lo = (r1 - mid.astype(F32)).astype(BF16)
    tri = tri_ref[...]
    cum = _dot(tri, hi) + _dot(tri, mid) + _dot(tri, lo)

    e_incl = jnp.exp(cum)
    a_t = (a * jnp.exp(cum - logw)).astype(BF16)
    r_t = (r * e_incl).astype(BF16)
    e_neg = jnp.exp(-cum)
    b_t = (b * e_neg).astype(BF16)
    k_t = (kd * e_neg).astype(BF16)
    v_b = v.astype(BF16)

    shape_c = (CHUNK, GROUP_LANES)
    ti = lax.broadcasted_iota(jnp.int32, shape_c, 0)
    ji = lax.broadcasted_iota(jnp.int32, shape_c, 1) % CHUNK
    strict = (ji > ti) if reverse else (ji < ti)
    incl = (ji >= ti) if reverse else (ji <= ti)
    eye = jnp.where(ji == ti, 1.0, 0.0).astype(F32)
    same16 = (ti // 16) == (ji // 16)
    same32 = (ti // 32) == (ji // 32)
    lvl0 = strict & same16
    lvl1 = strict & same32 & jnp.logical_not(same16)
    lvl2 = strict & jnp.logical_not(same32)
    bi = lax.broadcasted_iota(jnp.int32, (GROUP_LANES, GROUP_LANES), 0) // HEAD_DIM
    bj = lax.broadcasted_iota(jnp.int32, (GROUP_LANES, GROUP_LANES), 1) // HEAD_DIM
    bd_mask = bi == bj

    def mm(x, y):
        return _dot(x.astype(BF16), _block_diag(y.astype(BF16), bd_mask))

    n_chunks = tb // CHUNK
    n_groups = width // GROUP_LANES
    order = range(n_chunks - 1, -1, -1) if reverse else range(n_chunks)
    chains = [(c, gi) for c in range(n_chunks) for gi in range(n_groups)]

    def part(arr, ch):
        c, gi = ch
        return arr[c * CHUNK:(c + 1) * CHUNK, gi * GROUP_LANES:(gi + 1) * GROUP_LANES]

    def each(fn, *dicts):
        return {ch: fn(*(d[ch] for d in dicts)) for ch in chains}

    bd = lambda y: _block_diag(y, bd_mask)
    bdf = lambda y: jnp.where(bd_mask, y, 0.0)

    at_c = {ch: part(a_t, ch) for ch in chains}
    rt_c = {ch: part(r_t, ch) for ch in chains}
    bt_c = {ch: part(b_t, ch) for ch in chains}
    kt_c = {ch: part(k_t, ch) for ch in chains}
    v_c = {ch: part(v_b, ch) for ch in chains}
    ar = each(lambda p, q: jnp.concatenate([p, q], axis=0), at_c, rt_c)
    ab = each(lambda p, q: _dot_nt(p, bd(q)), ar, bt_c)
    ak = each(lambda p, q: _dot_nt(p, bd(q)), ar, kt_c)
    a_rb = each(lambda p: jnp.where(incl, p[CHUNK:], 0.0).astype(BF16), ab)
    a_kk = each(lambda p: jnp.concatenate([jnp.where(strict, p[0:CHUNK], 0.0),
                                           jnp.where(incl, p[CHUNK:], 0.0)], axis=0).astype(BF16), ak)
    akv = each(lambda p, q: _dot(p, bd(q)), a_kk, v_c)
    n0 = each(lambda p: jnp.where(lvl0, p[0:CHUNK], 0.0), ab)
    p2 = each(lambda p: mm(p, p), n0)
    x = each(lambda p: eye + p, n0)
    x = each(lambda p, q: p + mm(p, q), x, p2)
    p4 = each(lambda p: mm(p, p), p2)
    x = each(lambda p, q: p + mm(p, q), x, p4)
    p8 = each(lambda p: mm(p, p), p4)
    x = each(lambda p, q: p + mm(p, q), x, p8)
    for lvl in (lvl1, lvl2):
        y = each(lambda p, q: mm(p, jnp.where(lvl, q[0:CHUNK], 0.0)), x, ab)
        x = each(lambda p, q: p + mm(q, p), x, y)
    tinv = each(lambda p: p.astype(BF16), x)
    w_til = each(lambda p, q: _dot(p, bd(q)).astype(BF16), tinv, at_c)
    u_til = each(lambda p, q: _dot(p, bd(q[0:CHUNK].astype(BF16))).astype(BF16), tinv, akv)
    q_c = each(lambda r_, a_, w_: (r_.astype(F32) + _dot(a_, bd(w_))).astype(BF16), rt_c, a_rb, w_til)
    o_c = each(lambda a_, u_, k_: _dot(a_, bd(u_)) + k_[CHUNK:], a_rb, u_til, akv)

    def decay_row(ch):
        c, gi = ch
        edge = c * CHUNK if reverse else (c + 1) * CHUNK - 1
        return e_incl[edge:edge + 1, gi * GROUP_LANES:(gi + 1) * GROUP_LANES]

    p_c = {ch: decay_row(ch) for ch in chains}
    m_c = each(lambda w_, b_, p_: (bdf(_dot_tn(w_, b_)) * p_).astype(BF16), w_til, bt_c, p_c)
    c_c = each(lambda u_, v_, b_, k_, p_: bdf(_dot_tn(jnp.concatenate([u_, v_], axis=0),
                                                      jnp.concatenate([b_, k_], axis=0))) * p_,
               u_til, v_c, bt_c, kt_c, p_c)

    s = [state_ref[gi] for gi in range(n_groups)]
    for c in order:
        sb = [s[gi].astype(BF16) for gi in range(n_groups)]
        for gi in range(n_groups):
            o_ref[0, c * CHUNK:(c + 1) * CHUNK, gi * GROUP_LANES:(gi + 1) * GROUP_LANES] = (
                _dot_nt(q_c[(c, gi)], sb[gi]) + o_c[(c, gi)])
        s = [s[gi] * p_c[(c, gi)] + _dot(sb[gi], m_c[(c, gi)]) + c_c[(c, gi)] for gi in range(n_groups)]
    for gi in range(n_groups):
        state_ref[gi] = s[gi]


def _rwkv_call(u3, mu, wlr, w0, a0, k_k, k_a, r_k, ones_bd, tri, *, reverse, width):
    bsz, s, cols = u3.shape
    tb = SCAN_BLOCK
    nt = s // tb
    hb = tb // 8
    last8 = s // 8 - 1

    def blk(t):
        return (nt - 1 - t) if reverse else t

    kern = functools.partial(_rwkv_kernel, reverse=reverse, width=width)
    const2 = lambda b, t: (0, 0)
    out_sds = jax.ShapeDtypeStruct((bsz, s, width), F32)
    out_spec = pl.BlockSpec((1, tb, width), lambda b, t: (b, blk(t), 0))
    return pl.pallas_call(
        kern,
        grid=(bsz, nt),
        in_specs=[pl.BlockSpec((1, tb, cols), lambda b, t: (b, blk(t), 0)),
                  pl.BlockSpec((1, 8, cols), lambda b, t: (b, jnp.maximum(blk(t) * hb - 1, 0), 0)),
                  pl.BlockSpec((1, 8, cols), lambda b, t: (b, jnp.minimum((blk(t) + 1) * hb, last8), 0)),
                  pl.BlockSpec(mu.shape, const2),
                  pl.BlockSpec(wlr.shape, const2),
                  pl.BlockSpec((1, width), const2),
                  pl.BlockSpec((1, width), const2),
                  pl.BlockSpec((1, width), const2),
                  pl.BlockSpec((1, width), const2),
                  pl.BlockSpec((1, width), const2),
                  pl.BlockSpec(ones_bd.shape, const2),
                  pl.BlockSpec(tri.shape, const2)],
        out_specs=[out_spec, out_spec, out_spec],
        out_shape=[out_sds, out_sds, out_sds],
        scratch_shapes=[pltpu.VMEM((width // GROUP_LANES, GROUP_LANES, GROUP_LANES), F32)],
        compiler_params=_cparams(("parallel", "arbitrary")),
        name="rwkv_bwd" if reverse else "rwkv_fwd",
    )(u3, u3, u3, mu, wlr, w0, a0, k_k, k_a, r_k, ones_bd, tri)


def _na_kernel(q_ref, k_ref, v_ref, bias_ref, o_ref, *, rows):
    g = pl.program_id(1)
    band = NA_BAND_ROWS * GRID_W
    start_row = jnp.clip(g * NA_ROWS_PER_STEP - NA_KH // 2, 0, rows - NA_BAND_ROWS)
    start = pl.multiple_of(start_row * GRID_W, GRID_W)
    n_pairs = q_ref.shape[2] // LANES
    lane = lax.broadcasted_iota(jnp.int32, (band, LANES), 1)
    for p in range(n_pairs):
        ls = slice(p * LANES, (p + 1) * LANES)
        qp = q_ref[0, :, ls] * 0.125
        kp = k_ref[0, pl.ds(start, band), ls]
        vp = v_ref[0, pl.ds(start, band), ls]
        acc = None
        for hh in range(2):
            sel = (lane < HEAD_DIM) if hh == 0 else (lane >= HEAD_DIM)
            km = jnp.where(sel, kp, jnp.zeros_like(kp))
            vm = jnp.where(sel, vp, jnp.zeros_like(vp))
            sc = _dot_nt(qp, km) + bias_ref[0, 2 * p + hh]
            m = jnp.max(sc, axis=-1, keepdims=True)
            e = jnp.exp(sc - m)
            inv = 1.0 / jnp.sum(e, axis=-1, keepdims=True)
            part = _dot(e.astype(BF16), vm) * inv
            acc = part if acc is None else acc + part
        o_ref[0, :, ls] = acc.astype(o_ref.dtype)


def _na_call(qkv3, bias, width):
    bsz, s, _ = qkv3.shape
    rows = s // GRID_W
    tq = NA_ROWS_PER_STEP * GRID_W
    ng = rows // NA_ROWS_PER_STEP
    nh = width // HEAD_DIM

    def bias_idx(b, g):
        ty = jnp.where(g == 0, 0, jnp.where(g == ng - 1, 2, 1))
        return (ty, 0, 0, 0)

    return pl.pallas_call(
        functools.partial(_na_kernel, rows=rows),
        grid=(bsz, ng),
        in_specs=[pl.BlockSpec((1, tq, width), lambda b, g: (b, g, 0)),
                  pl.BlockSpec((1, s, width), lambda b, g: (b, 0, 1)),
                  pl.BlockSpec((1, s, width), lambda b, g: (b, 0, 2)),
                  pl.BlockSpec((1, nh, tq, NA_BAND_ROWS * GRID_W), bias_idx)],
        out_specs=pl.BlockSpec((1, tq, width), lambda b, g: (b, g, 0)),
        out_shape=jax.ShapeDtypeStruct((bsz, s, width), BF16),
        compiler_params=_cparams(("parallel", "arbitrary")),
        name="nbr_attn",
    )(qkv3, qkv3, qkv3, bias)


def _na_bias_tables(rpb, rows):
    ng = rows // NA_ROWS_PER_STEP
    kh = min(NA_KH, rows)
    nh, n_dr, n_dc = rpb.shape
    col = np.arange(GRID_W)
    cs = np.clip(col - NA_KW // 2, 0, GRID_W - NA_KW)
    ok_c = (col[None, :] >= cs[:, None]) & (col[None, :] < cs[:, None] + NA_KW)
    dc = col[None, :] - col[:, None] + (NA_KW - 1)
    onehot = (dc[None] == np.arange(n_dc)[:, None, None]) & ok_c[None]
    expand = jnp.asarray(onehot.reshape(n_dc, GRID_W * GRID_W), F32)
    by_dr = jnp.dot(rpb.reshape(nh * n_dr, n_dc), expand, precision=lax.Precision.HIGHEST)
    by_dr = jnp.where(ok_c, by_dr.reshape(nh, n_dr, GRID_W, GRID_W), NEG_BIG)
    outside = jnp.full((nh, GRID_W, GRID_W), NEG_BIG, F32)
    tables = []
    for g in (0, 1, ng - 1):
        bs = int(np.clip(g * NA_ROWS_PER_STEP - NA_KH // 2, 0, rows - NA_BAND_ROWS))
        q_rows = []
        for qi in range(g * NA_ROWS_PER_STEP, (g + 1) * NA_ROWS_PER_STEP):
            rs = int(np.clip(qi - kh // 2, 0, rows - kh))
            blocks = [by_dr[:, kr - qi + NA_KH - 1] if rs <= kr < rs + kh else outside
                      for kr in range(bs, bs + NA_BAND_ROWS)]
            q_rows.append(jnp.concatenate(blocks, axis=-1))
        tables.append(jnp.concatenate(q_rows, axis=1))
    return jnp.stack(tables)


def _merge_kernel(o0_ref, o1_ref, b0_ref, b1_ref, g_ref, yb_ref, gate_ref, x_ref, ones_ref,
                  gng_ref, gnb_ref, wa_ref, wb_ref, wo_ref, lg_ref, lb_ref, xo_ref, xob_ref, *,
                  alpha, d_model):
    ones_bd = ones_ref[...]
    inv_n = 1.0 / HEAD_DIM
    o = o0_ref[...] + o1_ref[...]
    mean = _dot(o.astype(BF16), ones_bd) * inv_n
    oc = o - mean
    var = _dot((oc * oc).astype(BF16), ones_bd) * inv_n
    on = oc * lax.rsqrt(var + GN_EPS) * gng_ref[...] + gnb_ref[...]
    ya = (on + b0_ref[...] + b1_ref[...]) * g_ref[...]
    gates = gate_ref[...].astype(F32)
    ga = _sigmoid(gates[:, 0:d_model])
    gb = _sigmoid(gates[:, d_model:])
    merged = ga * _dot(ya.astype(BF16), wa_ref[...]) + gb * _dot(yb_ref[...], wb_ref[...])
    y = alpha * x_ref[...] + _dot(merged.astype(BF16), wo_ref[...])
    xn = _layer_norm_rows(y, lg_ref[...], lb_ref[...])
    xo_ref[...] = xn
    xob_ref[...] = xn.astype(BF16)


def _merge_call(o0, o1, b0, b1, g, yb, gates, x, ones_bd, gn_g, gn_b, wa, wb, wo, ln_g, ln_b,
                alpha, tm):
    m, d = x.shape
    w = o0.shape[1]
    rowblk = lambda n: pl.BlockSpec((tm, n), lambda i: (i, 0))
    const = lambda shape: pl.BlockSpec(shape, lambda i: (0, 0))
    return pl.pallas_call(
        functools.partial(_merge_kernel, alpha=alpha, d_model=d),
        grid=(m // tm,),
        in_specs=[rowblk(w), rowblk(w), rowblk(w), rowblk(w), rowblk(w), rowblk(w),
                  rowblk(2 * d), rowblk(d), const(ones_bd.shape),
                  const((1, w)), const((1, w)), const(wa.shape), const(wb.shape), const(wo.shape),
                  const((1, d)), const((1, d))],
        out_specs=[rowblk(d), rowblk(d)],
        out_shape=[jax.ShapeDtypeStruct((m, d), F32), jax.ShapeDtypeStruct((m, d), BF16)],
        compiler_params=_cparams(("parallel",)),
        name="merge_out_ln",
    )(o0, o1, b0, b1, g, yb, gates, x, ones_bd, gn_g, gn_b, wa, wb, wo, ln_g, ln_b)


def _ffn_kernel(x_ref, xb_ref, wg_ref, wu_ref, wo_ref, lg_ref, lb_ref, xo_ref, xob_ref, *, alpha):
    xb = xb_ref[...]
    hg = _dot(xb, wg_ref[...])
    hu = _dot(xb, wu_ref[...])
    act = (hg * _sigmoid(hg) * hu).astype(BF16)
    y = alpha * x_ref[...] + _dot(act, wo_ref[...])
    xn = _layer_norm_rows(y, lg_ref[...], lb_ref[...])
    xo_ref[...] = xn
    xob_ref[...] = xn.astype(BF16)


def _ffn_call(x, xb, wg, wu, wo, ln_g, ln_b, alpha, tm):
    m, d = x.shape
    rowblk = pl.BlockSpec((tm, d), lambda i: (i, 0))
    const = lambda shape: pl.BlockSpec(shape, lambda i: (0, 0), pipeline_mode=pl.Buffered(1))
    return pl.pallas_call(
        functools.partial(_ffn_kernel, alpha=alpha),
        grid=(m // tm,),
        in_specs=[rowblk, rowblk, const(wg.shape), const(wu.shape), const(wo.shape),
                  pl.BlockSpec((1, d), lambda i: (0, 0)), pl.BlockSpec((1, d), lambda i: (0, 0))],
        out_specs=[rowblk, rowblk],
        out_shape=[jax.ShapeDtypeStruct((m, d), F32), jax.ShapeDtypeStruct((m, d), BF16)],
        compiler_params=_cparams(("parallel",)),
        name="swiglu_ln",
    )(x, xb, wg, wu, wo, ln_g, ln_b)


def _low_rank_weight(decay_up_d, iclr_up_d, gate_up, direction, width):
    r_dec, r_icl, r_gate = decay_up_d.shape[0], iclr_up_d.shape[0], gate_up.shape[0]
    w = jnp.zeros((2 * LANES, 3 * width), F32)
    w = w.at[direction * r_dec:(direction + 1) * r_dec, 0:width].set(decay_up_d)
    w = w.at[64 + direction * r_icl:64 + (direction + 1) * r_icl, width:2 * width].set(iclr_up_d)
    w = w.at[LANES:LANES + r_gate, 2 * width:].set(gate_up)
    return w.astype(BF16)


def kernel(x, ln_in_g, ln_in_b, w_in, shift_mu, decay_w0, decay_up, iclr_a0, iclr_up, gate_up,
           k_k, k_a, r_k, gn_g, gn_b, na_rpb, w_branch_rwkv, w_branch_na, w_out, ln1_g, ln1_b,
           w_ffn_in, w_ffn_out, ln2_g, ln2_b):
    bsz, s, d = x.shape
    depth = w_in.shape[0]
    width = k_k.shape[1]
    na_width = w_branch_na.shape[1]
    d_ff = w_ffn_out.shape[1]
    rwkv_cols = shift_mu.shape[2]
    rwkv_pad = 3 * width + 2 * LANES
    assert 2 * decay_up.shape[2] == 64 and 2 * iclr_up.shape[2] == 64 and gate_up.shape[1] <= LANES
    assert rwkv_cols == 3 * width + LANES + gate_up.shape[1] and s % SCAN_BLOCK == 0
    rows = s // GRID_W
    assert rows % NA_ROWS_PER_STEP == 0 and rows >= NA_BAND_ROWS and rows // NA_ROWS_PER_STEP >= 3
    alpha = (2.0 * depth) ** 0.25
    m = bsz * s
    tm = 512 if m % 512 == 0 else 256

    ones_bd = jnp.asarray(np.kron(np.eye(width // HEAD_DIM), np.ones((HEAD_DIM, HEAD_DIM))), BF16)
    pos = np.arange(SCAN_BLOCK)
    same_chunk = (pos[:, None] // CHUNK) == (pos[None, :] // CHUNK)
    tri_f = jnp.asarray(same_chunk & (pos[None, :] <= pos[:, None]), BF16)
    tri_r = jnp.asarray(same_chunk & (pos[None, :] >= pos[:, None]), BF16)

    x2, xb = _ln_call(x.reshape(m, d), ln_in_g, ln_in_b, tm)
    row = lambda p: p.reshape(1, -1)
    for l in range(depth):
        wl = w_in[l]
        w_r = jnp.pad(wl[:, :rwkv_cols], ((0, 0), (0, rwkv_pad - rwkv_cols))).astype(BF16)
        w_n = wl[:, rwkv_cols:rwkv_cols + 3 * na_width].astype(BF16)
        w_g = wl[:, rwkv_cols + 3 * na_width:].astype(BF16)
        mu = jnp.pad(shift_mu[l], ((0, 0), (0, rwkv_pad - rwkv_cols)))

        u, qkv, gates = _proj_call(xb, w_r, w_n, w_g, tm)
        u3 = u.reshape(bsz, s, rwkv_pad)

        outs = []
        for direction, tri in ((0, tri_f), (1, tri_r)):
            wlr = _low_rank_weight(decay_up[l, direction], iclr_up[l, direction], gate_up[l],
                                   direction, width)
            outs.append(_rwkv_call(
                u3, mu, wlr, row(decay_w0[l, direction]), row(iclr_a0[l, direction]),
                row(k_k[l]), row(k_a[l]), row(r_k[l]), ones_bd, tri,
                reverse=bool(direction), width=width))
        (o0, b0, g0), (o1, b1, _) = outs

        bias = _na_bias_tables(na_rpb[l], rows)
        yb = _na_call(qkv.reshape(bsz, s, 3 * na_width), bias, na_width)

        flat = lambda t: t.reshape(m, -1)
        x2, xb = _merge_call(
            flat(o0), flat(o1), flat(b0), flat(b1), flat(g0), flat(yb), gates, x2, ones_bd,
            row(gn_g[l]), row(gn_b[l]), w_branch_rwkv[l].astype(BF16), w_branch_na[l].astype(BF16),
            w_out[l].astype(BF16), row(ln1_g[l]), row(ln1_b[l]), alpha, tm)

        x2, xb = _ffn_call(
            x2, xb, w_ffn_in[l, :, :d_ff].astype(BF16), w_ffn_in[l, :, d_ff:].astype(BF16),
            w_ffn_out[l].astype(BF16), row(ln2_g[l]), row(ln2_b[l]), alpha, 256)
    return x2.reshape(bsz, s, d)
```

```python
import functools
import math

import jax
import jax.numpy as jnp
import numpy as np
from jax import lax
from jax.experimental import pallas as pl
from jax.experimental.pallas import tpu as pltpu

F32 = jnp.float32
BF16 = jnp.bfloat16

HEAD_DIM = 64
GRID_W = 64
NA_KH = 8
NA_KW = 16
NA_ROWS_PER_STEP = 4
NA_BAND_ROWS = NA_KH + NA_ROWS_PER_STEP - 1
LN_EPS = 1e-5
GN_EPS = 64e-5
NEG_BIG = -1e30
LANES = 128
CHUNK = 64
SCAN_BLOCK = 256
HEADS_PER_GROUP = 2
GROUP_LANES = HEADS_PER_GROUP * HEAD_DIM
VMEM_LIMIT = 56 * 1024 * 1024


def _cparams(sem):
    return pltpu.CompilerParams(dimension_semantics=sem, vmem_limit_bytes=VMEM_LIMIT)


def _dot(a, b):
    return jnp.dot(a, b, preferred_element_type=F32)


def _dot_nt(a, b):
    return lax.dot_general(a, b, (((1,), (1,)), ((), ())), preferred_element_type=F32)


def _dot_tn(a, b):
    return lax.dot_general(a, b, (((0,), (0,)), ((), ())), preferred_element_type=F32)


def _layer_norm_rows(x, g, b):
    mean = jnp.mean(x, axis=-1, keepdims=True)
    xc = x - mean
    var = jnp.mean(xc * xc, axis=-1, keepdims=True)
    return xc * lax.rsqrt(var + LN_EPS) * g + b


def _sigmoid(x):
    return 1.0 / (1.0 + jnp.exp(-x))


def _ln_kernel(x_ref, g_ref, b_ref, o_ref, ob_ref):
    y = _layer_norm_rows(x_ref[...], g_ref[...], b_ref[...])
    o_ref[...] = y
    ob_ref[...] = y.astype(BF16)


def _ln_call(x, g, b, tm):
    m, d = x.shape
    return pl.pallas_call(
        _ln_kernel,
        grid=(m // tm,),
        in_specs=[pl.BlockSpec((tm, d), lambda i: (i, 0)),
                  pl.BlockSpec((1, d), lambda i: (0, 0)),
                  pl.BlockSpec((1, d), lambda i: (0, 0))],
        out_specs=[pl.BlockSpec((tm, d), lambda i: (i, 0)),
                   pl.BlockSpec((tm, d), lambda i: (i, 0))],
        out_shape=[jax.ShapeDtypeStruct((m, d), F32), jax.ShapeDtypeStruct((m, d), BF16)],
        compiler_params=_cparams(("parallel",)),
        name="input_ln",
    )(x, g.reshape(1, d), b.reshape(1, d))


def _proj_kernel(x_ref, xp_ref, xn_ref, wr_ref, wn_ref, wg_ref, mu_ref, us_ref, qkv_ref, gate_ref, *,
                 blocks_per_seq):
    i = pl.program_id(0)
    x = x_ref[...]
    tm = x.shape[0]
    qkv_ref[...] = _dot(x, wn_ref[...]).astype(BF16)
    gate_ref[...] = _dot(x, wg_ref[...]).astype(BF16)
    h = xp_ref.shape[0]
    ext = _dot(jnp.concatenate([xp_ref[...], x, xn_ref[...]], axis=0), wr_ref[...])
    u = ext[h:h + tm]
    row = lax.broadcasted_iota(jnp.int32, u.shape, 0)
    first = jnp.logical_and(row == 0, i % blocks_per_seq == 0)
    last = jnp.logical_and(row == tm - 1, i % blocks_per_seq == blocks_per_seq - 1)
    u_prev = jnp.where(first, 0.0, ext[h - 1:h - 1 + tm])
    u_next = jnp.where(last, 0.0, ext[h + 1:h + 1 + tm])
    us_ref[...] = u + mu_ref[0:1, :] * (u_prev - u) + mu_ref[1:2, :] * (u_next - u)


def _proj_call(xb, w_rwkv, w_na, w_gate, mu, tm, seq):
    m, d = xb.shape
    nr, nn, ng = w_rwkv.shape[1], w_na.shape[1], w_gate.shape[1]
    const = lambda i: (0, 0)
    halo = 16
    hb = tm // halo
    last = m // halo - 1
    return pl.pallas_call(
        functools.partial(_proj_kernel, blocks_per_seq=seq // tm),
        grid=(m // tm,),
        in_specs=[pl.BlockSpec((tm, d), lambda i: (i, 0)),
                  pl.BlockSpec((halo, d), lambda i: (jnp.maximum(i * hb - 1, 0), 0)),
                  pl.BlockSpec((halo, d), lambda i: (jnp.minimum((i + 1) * hb, last), 0)),
                  pl.BlockSpec((d, nr), const),
                  pl.BlockSpec((d, nn), const),
                  pl.BlockSpec((d, ng), const),
                  pl.BlockSpec(mu.shape, const)],
        out_specs=[pl.BlockSpec((tm, nr), lambda i: (i, 0)),
                   pl.BlockSpec((tm, nn), lambda i: (i, 0)),
                   pl.BlockSpec((tm, ng), lambda i: (i, 0))],
        out_shape=[jax.ShapeDtypeStruct((m, nr), F32),
                   jax.ShapeDtypeStruct((m, nn), BF16),
                   jax.ShapeDtypeStruct((m, ng), BF16)],
        compiler_params=_cparams(("parallel",)),
        name="in_proj",
    )(xb, xb, xb, w_rwkv, w_na, w_gate, mu)


def _block_diag(x, bd_mask):
    tiled = jnp.concatenate([x] * HEADS_PER_GROUP, axis=0)
    return jnp.where(bd_mask, tiled, jnp.zeros_like(tiled))


def _rwkv_kernel(us_ref, wlr_ref, w0_ref, a0_ref, kk_ref, ka_ref, rk_ref, ones_ref, tri_ref,
                 o_ref, bonus_ref, g_ref, state_ref, *, reverse, width):
    t = pl.program_id(1)
    tb = us_ref.shape[1]
    c3 = 3 * width

    @pl.when(t == 0)
    def _():
        state_ref[...] = jnp.zeros_like(state_ref)

    us = us_ref[0]
    r = us[:, 0:width]
    k = us[:, width:2 * width]
    v = us[:, 2 * width:c3]
    low = us[:, c3:c3 + 2 * LANES]

    lane = lax.broadcasted_iota(jnp.int32, low.shape, 1)
    f = jnp.where(lane < 64, jnp.tanh(low), jnp.where(lane < LANES, low, _sigmoid(low)))
    lr = _dot(f.astype(BF16), wlr_ref[...])
    z = w0_ref[...] + lr[:, 0:width]
    iclr = _sigmoid(a0_ref[...] + lr[:, width:2 * width])
    g_ref[0] = lr[:, 2 * width:3 * width].astype(g_ref.dtype)

    logw = -math.exp(-0.5) * _sigmoid(z)

    ones_bd = ones_ref[...]
    kkr = k * kk_ref[...]
    ss = _dot((kkr * kkr).astype(BF16), ones_bd)
    kk = kkr * lax.rsqrt(jnp.maximum(ss, 1e-24))
    kd = k * (1.0 + (iclr - 1.0) * ka_ref[...])
    a = -kk
    b = kk * iclr

    bonus_ref[0] = (_dot((r * kd * rk_ref[...]).astype(BF16), ones_bd) * v).astype(bonus_ref.dtype)

    hi = logw.astype(BF16)
    r1 = logw - hi.astype(F32)
    mid = r1.astype(BF16)
    lo = (r1 - mid.astype(F32)).astype(BF16)
    tri = tri_ref[...]
    cum = _dot(tri, hi) + _dot(tri, mid) + _dot(tri, lo)

    e_incl = jnp.exp(cum)
    a_t = (a * jnp.exp(cum - logw)).astype(BF16)
    r_t = (r * e_incl).astype(BF16)
    e_neg = jnp.exp(-cum)
    b_t = (b * e_neg).astype(BF16)
    k_t = (kd * e_neg).astype(BF16)
    v_b = v.astype(BF16)

    shape_c = (CHUNK, GROUP_LANES)
    ti = lax.broadcasted_iota(jnp.int32, shape_c, 0)
    ji = lax.broadcasted_iota(jnp.int32, shape_c, 1) % CHUNK
    strict = (ji > ti) if reverse else (ji < ti)
    incl = (ji >= ti) if reverse else (ji <= ti)
    eye = jnp.where(ji == ti, 1.0, 0.0).astype(F32)
    same16 = (ti // 16) == (ji // 16)
    same32 = (ti // 32) == (ji // 32)
    lvl0 = strict & same16
    lvl1 = strict & same32 & jnp.logical_not(same16)
    lvl2 = strict & jnp.logical_not(same32)
    bi = lax.broadcasted_iota(jnp.int32, (GROUP_LANES, GROUP_LANES), 0) // HEAD_DIM
    bj = lax.broadcasted_iota(jnp.int32, (GROUP_LANES, GROUP_LANES), 1) // HEAD_DIM
    bd_mask = bi == bj

    def mm(x, y):
        return _dot(x.astype(BF16), _block_diag(y.astype(BF16), bd_mask))

    n_chunks = tb // CHUNK
    n_groups = width // GROUP_LANES
    order = range(n_chunks - 1, -1, -1) if reverse else range(n_chunks)
    chains = [(c, gi) for c in range(n_chunks) for gi in range(n_groups)]

    def part(arr, ch):
        c, gi = ch
        return arr[c * CHUNK:(c + 1) * CHUNK, gi * GROUP_LANES:(gi + 1) * GROUP_LANES]

    def each(fn, *dicts):
        return {ch: fn(*(d[ch] for d in dicts)) for ch in chains}

    bd = lambda y: _block_diag(y, bd_mask)
    bdf = lambda y: jnp.where(bd_mask, y, 0.0)

    at_c = {ch: part(a_t, ch) for ch in chains}
    rt_c = {ch: part(r_t, ch) for ch in chains}
    bk_c = {ch: jnp.concatenate([part(b_t, ch), part(k_t, ch)], axis=0) for ch in chains}
    v_c = {ch: part(v_b, ch) for ch in chains}
    ar = each(lambda p, q: jnp.concatenate([p, q], axis=0), at_c, rt_c)
    ab = each(lambda p, q: _dot_nt(p, bd(q[0:CHUNK])), ar, bk_c)
    ak = each(lambda p, q: _dot_nt(p, bd(q[CHUNK:])), ar, bk_c)
    a_rb = each(lambda p: jnp.where(incl, p[CHUNK:], 0.0).astype(BF16), ab)
    a_kk = each(lambda p: jnp.concatenate([jnp.where(strict, p[0:CHUNK], 0.0),
                                           jnp.where(incl, p[CHUNK:], 0.0)], axis=0).astype(BF16), ak)
    akv = each(lambda p, q: _dot(p, bd(q)), a_kk, v_c)
    n0 = each(lambda p: jnp.where(lvl0, p[0:CHUNK], 0.0), ab)
    pw = each(lambda p: mm(p, p), n0)
    x = each(lambda p: eye + p, n0)
    for _ in range(2):
        both = each(lambda p, q: mm(jnp.concatenate([p, q], axis=0), q), x, pw)
        x = each(lambda p, q: p + q[0:CHUNK], x, both)
        pw = each(lambda q: q[CHUNK:], both)
    x = each(lambda p, q: p + mm(p, q), x, pw)
    for lvl in (lvl1, lvl2):
        y = each(lambda p, q: mm(p, jnp.where(lvl, q[0:CHUNK], 0.0)), x, ab)
        x = each(lambda p, q: p + mm(q, p), x, y)
    tinv = each(lambda p: p.astype(BF16), x)
    wu = each(lambda p, a_, k_: _dot(p, jnp.concatenate([bd(a_), bd(k_[0:CHUNK].astype(BF16))], axis=1))
              .astype(BF16), tinv, at_c, akv)
    qo = each(lambda a_, w_: _dot(a_, jnp.concatenate([bd(w_[:, 0:GROUP_LANES]), bd(w_[:, GROUP_LANES:])],
                                                      axis=1)), a_rb, wu)
    q_c = each(lambda r_, p: (r_.astype(F32) + p[:, 0:GROUP_LANES]).astype(BF16), rt_c, qo)
    o_c = each(lambda p, k_: p[:, GROUP_LANES:] + k_[CHUNK:], qo, akv)
    zeros_c = jnp.zeros((CHUNK, GROUP_LANES), BF16)
    mc = each(lambda b_, w_, v_: _dot_tn(b_, jnp.concatenate(
        [w_, jnp.concatenate([zeros_c, v_], axis=1)], axis=0)), bk_c, wu, v_c)

    edges = [c * CHUNK if reverse else (c + 1) * CHUNK - 1 for c in range(n_chunks)]
    edge_rows = jnp.concatenate([e_incl[e:e + 1, :] for e in edges]
                                + [jnp.zeros((8 - n_chunks, width), F32)], axis=0)
    edge_cols = edge_rows.T
    p_c = {(c, gi): edge_cols[gi * GROUP_LANES:(gi + 1) * GROUP_LANES, c:c + 1] for c, gi in chains}
    m_c = each(lambda p, d: (bdf(p[:, 0:GROUP_LANES]) * d).astype(BF16), mc, p_c)
    c_c = each(lambda p, d: bdf(p[:, GROUP_LANES:]) * d, mc, p_c)

    st = [state_ref[gi] for gi in range(n_groups)]
    for c in order:
        for gi in range(n_groups):
            ch = (c, gi)
            both = _dot(jnp.concatenate([m_c[ch], q_c[ch]], axis=0), st[gi].astype(BF16))
            o_ref[0, c * CHUNK:(c + 1) * CHUNK, gi * GROUP_LANES:(gi + 1) * GROUP_LANES] = (
                both[GROUP_LANES:] + o_c[ch]).astype(o_ref.dtype)
            st[gi] = st[gi] * p_c[ch] + both[0:GROUP_LANES] + c_c[ch]
    for gi in range(n_groups):
        state_ref[gi] = st[gi]


def _rwkv_call(us3, wlr, w0, a0, k_k, k_a, r_k, ones_bd, tri, *, reverse, width):
    bsz, s, cols = us3.shape
    tb = SCAN_BLOCK
    nt = s // tb

    def blk(t):
        return (nt - 1 - t) if reverse else t

    kern = functools.partial(_rwkv_kernel, reverse=reverse, width=width)
    const2 = lambda b, t: (0, 0)
    out_sds = jax.ShapeDtypeStruct((bsz, s, width), BF16)
    out_spec = pl.BlockSpec((1, tb, width), lambda b, t: (b, blk(t), 0))
    return pl.pallas_call(
        kern,
        grid=(bsz, nt),
        in_specs=[pl.BlockSpec((1, tb, cols), lambda b, t: (b, blk(t), 0)),
                  pl.BlockSpec(wlr.shape, const2),
                  pl.BlockSpec((1, width), const2),
                  pl.BlockSpec((1, width), const2),
                  pl.BlockSpec((1, width), const2),
                  pl.BlockSpec((1, width), const2),
                  pl.BlockSpec((1, width), const2),
                  pl.BlockSpec(ones_bd.shape, const2),
                  pl.BlockSpec(tri.shape, const2)],
        out_specs=[out_spec, out_spec, out_spec],
        out_shape=[out_sds, out_sds, out_sds],
        scratch_shapes=[pltpu.VMEM((width // GROUP_LANES, GROUP_LANES, GROUP_LANES), F32)],
        compiler_params=_cparams(("parallel", "arbitrary")),
        name="rwkv_bwd" if reverse else "rwkv_fwd",
    )(us3, wlr, w0, a0, k_k, k_a, r_k, ones_bd, tri)


def _na_kernel(q_ref, k_ref, v_ref, bias_ref, o_ref, *, rows):
    g = pl.program_id(1)
    band = NA_BAND_ROWS * GRID_W
    start_row = jnp.clip(g * NA_ROWS_PER_STEP - NA_KH // 2, 0, rows - NA_BAND_ROWS)
    start = pl.multiple_of(start_row * GRID_W, GRID_W)
    n_pairs = q_ref.shape[2] // LANES
    lane = lax.broadcasted_iota(jnp.int32, (band, LANES), 1)
    for p in range(n_pairs):
        ls = slice(p * LANES, (p + 1) * LANES)
        qp = q_ref[0, :, ls] * 0.125
        kp = k_ref[0, pl.ds(start, band), ls]
        vp = v_ref[0, pl.ds(start, band), ls]
        acc = None
        for hh in range(2):
            sel = (lane < HEAD_DIM) if hh == 0 else (lane >= HEAD_DIM)
            km = jnp.where(sel, kp, jnp.zeros_like(kp))
            vm = jnp.where(sel, vp, jnp.zeros_like(vp))
            sc = _dot_nt(qp, km) + bias_ref[0, 2 * p + hh]
            m = jnp.max(sc, axis=-1, keepdims=True)
            e = jnp.exp(sc - m)
            inv = 1.0 / jnp.sum(e, axis=-1, keepdims=True)
            part = _dot(e.astype(BF16), vm) * inv
            acc = part if acc is None else acc + part
        o_ref[0, :, ls] = acc.astype(o_ref.dtype)


def _na_call(qkv3, bias, width):
    bsz, s, _ = qkv3.shape
    rows = s // GRID_W
    tq = NA_ROWS_PER_STEP * GRID_W
    ng = rows // NA_ROWS_PER_STEP
    nh = width // HEAD_DIM

    def bias_idx(b, g):
        ty = jnp.where(g == 0, 0, jnp.where(g == ng - 1, 2, 1))
        return (ty, 0, 0, 0)

    return pl.pallas_call(
        functools.partial(_na_kernel, rows=rows),
        grid=(bsz, ng),
        in_specs=[pl.BlockSpec((1, tq, width), lambda b, g: (b, g, 0)),
                  pl.BlockSpec((1, s, width), lambda b, g: (b, 0, 1)),
                  pl.BlockSpec((1, s, width), lambda b, g: (b, 0, 2)),
                  pl.BlockSpec((1, nh, tq, NA_BAND_ROWS * GRID_W), bias_idx)],
        out_specs=pl.BlockSpec((1, tq, width), lambda b, g: (b, g, 0)),
        out_shape=jax.ShapeDtypeStruct((bsz, s, width), BF16),
        compiler_params=_cparams(("parallel", "arbitrary")),
        name="nbr_attn",
    )(qkv3, qkv3, qkv3, bias)


def _na_bias_tables(rpb, rows):
    ng = rows // NA_ROWS_PER_STEP
    kh = min(NA_KH, rows)
    nh, n_dr, n_dc = rpb.shape
    col = np.arange(GRID_W)
    cs = np.clip(col - NA_KW // 2, 0, GRID_W - NA_KW)
    ok_c = (col[None, :] >= cs[:, None]) & (col[None, :] < cs[:, None] + NA_KW)
    dc = col[None, :] - col[:, None] + (NA_KW - 1)
    onehot = (dc[None] == np.arange(n_dc)[:, None, None]) & ok_c[None]
    expand = jnp.asarray(onehot.reshape(n_dc, GRID_W * GRID_W), F32)
    by_dr = jnp.dot(rpb.reshape(nh * n_dr, n_dc), expand, precision=lax.Precision.HIGHEST)
    by_dr = jnp.where(ok_c, by_dr.reshape(nh, n_dr, GRID_W, GRID_W), NEG_BIG)
    outside = jnp.full((nh, GRID_W, GRID_W), NEG_BIG, F32)
    tables = []
    for g in (0, 1, ng - 1):
        bs = int(np.clip(g * NA_ROWS_PER_STEP - NA_KH // 2, 0, rows - NA_BAND_ROWS))
        q_rows = []
        for qi in range(g * NA_ROWS_PER_STEP, (g + 1) * NA_ROWS_PER_STEP):
            rs = int(np.clip(qi - kh // 2, 0, rows - kh))
            blocks = [by_dr[:, kr - qi + NA_KH - 1] if rs <= kr < rs + kh else outside
                      for kr in range(bs, bs + NA_BAND_ROWS)]
            q_rows.append(jnp.concatenate(blocks, axis=-1))
        tables.append(jnp.concatenate(q_rows, axis=1))
    return jnp.stack(tables)


def _merge_kernel(o0_ref, o1_ref, b0_ref, b1_ref, g_ref, yb_ref, gate_ref, x_ref, ones_ref,
                  gng_ref, gnb_ref, wa_ref, wb_ref, wo_ref, lg_ref, lb_ref, xo_ref, xob_ref, *,
                  alpha, d_model):
    ones_bd = ones_ref[...]
    inv_n = 1.0 / HEAD_DIM
    o = o0_ref[...].astype(F32) + o1_ref[...].astype(F32)
    mean = _dot(o.astype(BF16), ones_bd) * inv_n
    oc = o - mean
    var = _dot((oc * oc).astype(BF16), ones_bd) * inv_n
    on = oc * lax.rsqrt(var + GN_EPS) * gng_ref[...] + gnb_ref[...]
    ya = (on + b0_ref[...].astype(F32) + b1_ref[...].astype(F32)) * g_ref[...].astype(F32)
    gates = gate_ref[...].astype(F32)
    ga = _sigmoid(gates[:, 0:d_model])
    gb = _sigmoid(gates[:, d_model:])
    merged = ga * _dot(ya.astype(BF16), wa_ref[...]) + gb * _dot(yb_ref[...], wb_ref[...])
    y = alpha * x_ref[...] + _dot(merged.astype(BF16), wo_ref[...])
    xn = _layer_norm_rows(y, lg_ref[...], lb_ref[...])
    xo_ref[...] = xn
    xob_ref[...] = xn.astype(BF16)


def _merge_call(o0, o1, b0, b1, g, yb, gates, x, ones_bd, gn_g, gn_b, wa, wb, wo, ln_g, ln_b,
                alpha, tm):
    m, d = x.shape
    w = o0.shape[1]
    rowblk = lambda n: pl.BlockSpec((tm, n), lambda i: (i, 0))
    const = lambda shape: pl.BlockSpec(shape, lambda i: (0, 0))
    return pl.pallas_call(
        functools.partial(_merge_kernel, alpha=alpha, d_model=d),
        grid=(m // tm,),
        in_specs=[rowblk(w), rowblk(w), rowblk(w), rowblk(w), rowblk(w), rowblk(w),
                  rowblk(2 * d), rowblk(d), const(ones_bd.shape),
                  const((1, w)), const((1, w)), const(wa.shape), const(wb.shape), const(wo.shape),
                  const((1, d)), const((1, d))],
        out_specs=[rowblk(d), rowblk(d)],
        out_shape=[jax.ShapeDtypeStruct((m, d), F32), jax.ShapeDtypeStruct((m, d), BF16)],
        compiler_params=_cparams(("parallel",)),
        name="merge_out_ln",
    )(o0, o1, b0, b1, g, yb, gates, x, ones_bd, gn_g, gn_b, wa, wb, wo, ln_g, ln_b)


def _ffn_kernel(x_ref, xb_ref, wg_ref, wu_ref, wo_ref, lg_ref, lb_ref, xo_ref, xob_ref, *, alpha):
    xb = xb_ref[...]
    hg = _dot(xb, wg_ref[...])
    hu = _dot(xb, wu_ref[...])
    act = (hg * _sigmoid(hg) * hu).astype(BF16)
    y = alpha * x_ref[...] + _dot(act, wo_ref[...])
    xn = _layer_norm_rows(y, lg_ref[...], lb_ref[...])
    xo_ref[...] = xn
    xob_ref[...] = xn.astype(BF16)


def _ffn_call(x, xb, wg, wu, wo, ln_g, ln_b, alpha, tm):
    m, d = x.shape
    rowblk = pl.BlockSpec((tm, d), lambda i: (i, 0))
    const = lambda shape: pl.BlockSpec(shape, lambda i: (0, 0), pipeline_mode=pl.Buffered(1))
    return pl.pallas_call(
        functools.partial(_ffn_kernel, alpha=alpha),
        grid=(m // tm,),
        in_specs=[rowblk, rowblk, const(wg.shape), const(wu.shape), const(wo.shape),
                  pl.BlockSpec((1, d), lambda i: (0, 0)), pl.BlockSpec((1, d), lambda i: (0, 0))],
        out_specs=[rowblk, rowblk],
        out_shape=[jax.ShapeDtypeStruct((m, d), F32), jax.ShapeDtypeStruct((m, d), BF16)],
        compiler_params=_cparams(("parallel",)),
        name="swiglu_ln",
    )(x, xb, wg, wu, wo, ln_g, ln_b)


def _low_rank_weight(decay_up_d, iclr_up_d, gate_up, direction, width):
    r_dec, r_icl, r_gate = decay_up_d.shape[0], iclr_up_d.shape[0], gate_up.shape[0]
    w = jnp.zeros((2 * LANES, 3 * width), F32)
    w = w.at[direction * r_dec:(direction + 1) * r_dec, 0:width].set(decay_up_d)
    w = w.at[64 + direction * r_icl:64 + (direction + 1) * r_icl, width:2 * width].set(iclr_up_d)
    w = w.at[LANES:LANES + r_gate, 2 * width:].set(gate_up)
    return w.astype(BF16)


def kernel(x, ln_in_g, ln_in_b, w_in, shift_mu, decay_w0, decay_up, iclr_a0, iclr_up, gate_up,
           k_k, k_a, r_k, gn_g, gn_b, na_rpb, w_branch_rwkv, w_branch_na, w_out, ln1_g, ln1_b,
           w_ffn_in, w_ffn_out, ln2_g, ln2_b):
    bsz, s, d = x.shape
    depth = w_in.shape[0]
    width = k_k.shape[1]
    na_width = w_branch_na.shape[1]
    d_ff = w_ffn_out.shape[1]
    rwkv_cols = shift_mu.shape[2]
    rwkv_pad = 3 * width + 2 * LANES
    assert 2 * decay_up.shape[2] == 64 and 2 * iclr_up.shape[2] == 64 and gate_up.shape[1] <= LANES
    assert rwkv_cols == 3 * width + LANES + gate_up.shape[1] and s % SCAN_BLOCK == 0
    rows = s // GRID_W
    assert rows % NA_ROWS_PER_STEP == 0 and rows >= NA_BAND_ROWS and rows // NA_ROWS_PER_STEP >= 3
    alpha = (2.0 * depth) ** 0.25
    m = bsz * s
    tm = 512 if m % 512 == 0 else 256

    ones_bd = jnp.asarray(np.kron(np.eye(width // HEAD_DIM), np.ones((HEAD_DIM, HEAD_DIM))), BF16)
    pos = np.arange(SCAN_BLOCK)
    same_chunk = (pos[:, None] // CHUNK) == (pos[None, :] // CHUNK)
    tri_f = jnp.asarray(same_chunk & (pos[None, :] <= pos[:, None]), BF16)
    tri_r = jnp.asarray(same_chunk & (pos[None, :] >= pos[:, None]), BF16)

    x2, xb = _ln_call(x.reshape(m, d), ln_in_g, ln_in_b, tm)
    row = lambda p: p.reshape(1, -1)
    for l in range(depth):
        wl = w_in[l]
        w_r = jnp.pad(wl[:, :rwkv_cols], ((0, 0), (0, rwkv_pad - rwkv_cols))).astype(BF16)
        w_n = wl[:, rwkv_cols:rwkv_cols + 3 * na_width].astype(BF16)
        w_g = wl[:, rwkv_cols + 3 * na_width:].astype(BF16)
        mu = jnp.pad(shift_mu[l], ((0, 0), (0, rwkv_pad - rwkv_cols)))

        us, qkv, gates = _proj_call(xb, w_r, w_n, w_g, mu, tm, s)
        us3 = us.reshape(bsz, s, rwkv_pad)

        outs = []
        for direction, tri in ((0, tri_f), (1, tri_r)):
            wlr = _low_rank_weight(decay_up[l, direction], iclr_up[l, direction], gate_up[l],
                                   direction, width)
            outs.append(_rwkv_call(
                us3, wlr, row(decay_w0[l, direction]), row(iclr_a0[l, direction]),
                row(k_k[l]), row(k_a[l]), row(r_k[l]), ones_bd, tri,
                reverse=bool(direction), width=width))
        (o0, b0, g0), (o1, b1, _) = outs

        bias = _na_bias_tables(na_rpb[l], rows)
        yb = _na_call(qkv.reshape(bsz, s, 3 * na_width), bias, na_width)

        flat = lambda t: t.reshape(m, -1)
        x2, xb = _merge_call(
            flat(o0), flat(o1), flat(b0), flat(b1), flat(g0), flat(yb), gates, x2, ones_bd,
            row(gn_g[l]), row(gn_b[l]), w_branch_rwkv[l].astype(BF16), w_branch_na[l].astype(BF16),
            w_out[l].astype(BF16), row(ln1_g[l]), row(ln1_b[l]), alpha, tm)

        x2, xb = _ffn_call(
            x2, xb, w_ffn_in[l, :, :d_ff].astype(BF16), w_ffn_in[l, :, d_ff:].astype(BF16),
            w_ffn_out[l].astype(BF16), row(ln2_g[l]), row(ln2_b[l]), alpha, 256)
    return x2.reshape(bsz, s, d)
```

```python
import functools
import math

import jax
import jax.numpy as jnp
import numpy as np
from jax import lax
from jax.experimental import pallas as pl
from jax.experimental.pallas import tpu as pltpu

F32 = jnp.float32
BF16 = jnp.bfloat16

HEAD_DIM = 64
GRID_W = 64
NA_KH = 8
NA_KW = 16
NA_ROWS_PER_STEP = 4
NA_BAND_ROWS = NA_KH + NA_ROWS_PER_STEP - 1
LN_EPS = 1e-5
GN_EPS = 64e-5
NEG_BIG = -1e30
LOG2E = math.log2(math.e)
LANES = 128
MXU_WIDTH = 256
CHUNK = 64
SCAN_BLOCK = 256
HEADS_PER_GROUP = 2
GROUP_LANES = HEADS_PER_GROUP * HEAD_DIM
VMEM_LIMIT = 56 * 1024 * 1024


def _cparams(sem):
    return pltpu.CompilerParams(dimension_semantics=sem, vmem_limit_bytes=VMEM_LIMIT)


def _dot(a, b):
    return jnp.dot(a, b, preferred_element_type=F32)


def _dot_nt(a, b):
    return lax.dot_general(a, b, (((1,), (1,)), ((), ())), preferred_element_type=F32)


def _dot_tn(a, b):
    return lax.dot_general(a, b, (((0,), (0,)), ((), ())), preferred_element_type=F32)


def _layer_norm_rows(x, g, b):
    mean = jnp.mean(x, axis=-1, keepdims=True)
    xc = x - mean
    var = jnp.mean(xc * xc, axis=-1, keepdims=True)
    return xc * lax.rsqrt(var + LN_EPS) * g + b


def _sigmoid(x):
    return 1.0 / (1.0 + jnp.exp(-x))


def _ln_kernel(x_ref, g_ref, b_ref, o_ref, ob_ref):
    y = _layer_norm_rows(x_ref[...], g_ref[...], b_ref[...])
    o_ref[...] = y
    ob_ref[...] = y.astype(BF16)


def _ln_call(x, g, b, tm):
    m, d = x.shape
    return pl.pallas_call(
        _ln_kernel,
        grid=(m // tm,),
        in_specs=[pl.BlockSpec((tm, d), lambda i: (i, 0)),
                  pl.BlockSpec((1, d), lambda i: (0, 0)),
                  pl.BlockSpec((1, d), lambda i: (0, 0))],
        out_specs=[pl.BlockSpec((tm, d), lambda i: (i, 0)),
                   pl.BlockSpec((tm, d), lambda i: (i, 0))],
        out_shape=[jax.ShapeDtypeStruct((m, d), F32), jax.ShapeDtypeStruct((m, d), BF16)],
        compiler_params=_cparams(("parallel",)),
        name="input_ln",
    )(x, g.reshape(1, d), b.reshape(1, d))


def _proj_kernel(x_ref, xp_ref, xn_ref, wr_ref, wn_ref, wg_ref, mu_ref, us_ref, qkv_ref, gate_ref, *,
                 blocks_per_seq):
    i = pl.program_id(0)
    x = x_ref[...]
    tm = x.shape[0]
    qkv_ref[...] = _dot(x, wn_ref[...]).astype(BF16)
    gate_ref[...] = _dot(x, wg_ref[...]).astype(BF16)
    h = xp_ref.shape[0]
    ext = _dot(jnp.concatenate([xp_ref[...], x, xn_ref[...]], axis=0), wr_ref[...])
    u = ext[h:h + tm]
    row = lax.broadcasted_iota(jnp.int32, u.shape, 0)
    first = jnp.logical_and(row == 0, i % blocks_per_seq == 0)
    last = jnp.logical_and(row == tm - 1, i % blocks_per_seq == blocks_per_seq - 1)
    u_prev = jnp.where(first, 0.0, ext[h - 1:h - 1 + tm])
    u_next = jnp.where(last, 0.0, ext[h + 1:h + 1 + tm])
    us_ref[...] = u + mu_ref[0:1, :] * (u_prev - u) + mu_ref[1:2, :] * (u_next - u)


def _proj_call(xb, w_rwkv, w_na, w_gate, mu, tm, seq):
    m, d = xb.shape
    nr, nn, ng = w_rwkv.shape[1], w_na.shape[1], w_gate.shape[1]
    const = lambda i: (0, 0)
    halo = 16
    hb = tm // halo
    last = m // halo - 1
    return pl.pallas_call(
        functools.partial(_proj_kernel, blocks_per_seq=seq // tm),
        grid=(m // tm,),
        in_specs=[pl.BlockSpec((tm, d), lambda i: (i, 0)),
                  pl.BlockSpec((halo, d), lambda i: (jnp.maximum(i * hb - 1, 0), 0)),
                  pl.BlockSpec((halo, d), lambda i: (jnp.minimum((i + 1) * hb, last), 0)),
                  pl.BlockSpec((d, nr), const),
                  pl.BlockSpec((d, nn), const),
                  pl.BlockSpec((d, ng), const),
                  pl.BlockSpec(mu.shape, const)],
        out_specs=[pl.BlockSpec((tm, nr), lambda i: (i, 0)),
                   pl.BlockSpec((tm, nn), lambda i: (i, 0)),
                   pl.BlockSpec((tm, ng), lambda i: (i, 0))],
        out_shape=[jax.ShapeDtypeStruct((m, nr), F32),
                   jax.ShapeDtypeStruct((m, nn), BF16),
                   jax.ShapeDtypeStruct((m, ng), BF16)],
        compiler_params=_cparams(("parallel",)),
        name="in_proj",
    )(xb, xb, xb, w_rwkv, w_na, w_gate, mu)


def _block_diag(x, bd_mask):
    tiled = jnp.concatenate([x] * HEADS_PER_GROUP, axis=0)
    return jnp.where(bd_mask, tiled, jnp.zeros_like(tiled))


def _head_sums(xs, ones_blocks):
    rows, width = xs[0].shape
    gw = ones_blocks.shape[0]
    n_g = width // gw
    stacked = jnp.concatenate([x[:, i * gw:(i + 1) * gw] for x in xs for i in range(n_g)], axis=0)
    res = _dot(stacked.astype(BF16), ones_blocks)
    return [jnp.concatenate([res[(j * n_g + i) * rows:(j * n_g + i + 1) * rows] for i in range(n_g)],
                            axis=1) for j in range(len(xs))]


def _rwkv_kernel(us_ref, wdi_ref, wgate_ref, w0_ref, a0_ref, kk_ref, ka_ref, rk_ref, ones_ref, tri_ref,
                 o_ref, bonus_ref, *rest, reverse, width):
    g_ref = None if reverse else rest[0]
    state_ref = rest[-1]
    t = pl.program_id(1)
    tb = us_ref.shape[1]
    c3 = 3 * width

    @pl.when(t == 0)
    def _():
        state_ref[...] = jnp.zeros_like(state_ref)

    us = us_ref[0]
    r = us[:, 0:width]
    k = us[:, width:2 * width]
    v = us[:, 2 * width:c3]
    low = us[:, c3:c3 + 2 * LANES]

    lane = lax.broadcasted_iota(jnp.int32, (tb, LANES), 1)
    f_di = jnp.where(lane < 64, jnp.tanh(low[:, 0:LANES]), low[:, 0:LANES]).astype(BF16)
    lr = _dot(f_di, wdi_ref[...])
    z = w0_ref[...] + lr[:, 0:width]
    iclr = _sigmoid(a0_ref[...] + lr[:, width:2 * width])
    if g_ref is not None:
        g_ref[0] = _dot(_sigmoid(low[:, LANES:]).astype(BF16), wgate_ref[...]).astype(g_ref.dtype)

    logw = -math.exp(-0.5) * _sigmoid(z)

    kkr = k * kk_ref[...]
    kd = k * (1.0 + (iclr - 1.0) * ka_ref[...])
    ss, rkd = _head_sums([kkr * kkr, r * kd * rk_ref[...]], ones_ref[...])
    kk = kkr * lax.rsqrt(jnp.maximum(ss, 1e-24))
    a = -kk
    b = kk * iclr
    bonus_ref[0] = (rkd * v).astype(bonus_ref.dtype)

    hi = logw.astype(BF16)
    lo = (logw - hi.astype(F32)).astype(BF16)
    tri = tri_ref[...]
    cum = _dot(tri, hi) + _dot(tri, lo)

    e_incl = jnp.exp(cum)
    a_t = (a * jnp.exp(cum - logw)).astype(BF16)
    r_t = (r * e_incl).astype(BF16)
    e_neg = jnp.exp(-cum)
    b_t = (b * e_neg).astype(BF16)
    k_t = (kd * e_neg).astype(BF16)
    v_b = v.astype(BF16)

    shape_c = (CHUNK, GROUP_LANES)
    ti = lax.broadcasted_iota(jnp.int32, shape_c, 0)
    ji = lax.broadcasted_iota(jnp.int32, shape_c, 1) % CHUNK
    strict = (ji > ti) if reverse else (ji < ti)
    incl = (ji >= ti) if reverse else (ji <= ti)
    eye = jnp.where(ji == ti, 1.0, 0.0).astype(F32)
    same16 = (ti // 16) == (ji // 16)
    same32 = (ti // 32) == (ji // 32)
    lvl0 = strict & same16
    lvl1 = strict & same32 & jnp.logical_not(same16)
    lvl2 = strict & jnp.logical_not(same32)
    bi = lax.broadcasted_iota(jnp.int32, (GROUP_LANES, GROUP_LANES), 0) // HEAD_DIM
    bj = lax.broadcasted_iota(jnp.int32, (GROUP_LANES, GROUP_LANES), 1) // HEAD_DIM
    bd_mask = bi == bj

    def mm(x, y):
        return _dot(x.astype(BF16), _block_diag(y.astype(BF16), bd_mask))

    n_chunks = tb // CHUNK
    n_groups = width // GROUP_LANES
    order = range(n_chunks - 1, -1, -1) if reverse else range(n_chunks)
    chains = [(c, gi) for c in range(n_chunks) for gi in range(n_groups)]

    def part(arr, ch):
        c, gi = ch
        return arr[c * CHUNK:(c + 1) * CHUNK, gi * GROUP_LANES:(gi + 1) * GROUP_LANES]

    def each(fn, *dicts):
        return {ch: fn(*(d[ch] for d in dicts)) for ch in chains}

    bd = lambda y: _block_diag(y, bd_mask)
    bdf = lambda y: jnp.where(bd_mask, y, 0.0)

    at_c = {ch: part(a_t, ch) for ch in chains}
    rt_c = {ch: part(r_t, ch) for ch in chains}
    bk_c = {ch: jnp.concatenate([part(b_t, ch), part(k_t, ch)], axis=0) for ch in chains}
    v_c = {ch: part(v_b, ch) for ch in chains}
    ar = each(lambda p, q: jnp.concatenate([p, q], axis=0), at_c, rt_c)
    ab = each(lambda p, q: _dot_nt(p, bd(q[0:CHUNK])), ar, bk_c)
    ak = each(lambda p, q: _dot_nt(p, bd(q[CHUNK:])), ar, bk_c)
    a_rb = each(lambda p: jnp.where(incl, p[CHUNK:], 0.0).astype(BF16), ab)
    a_kk = each(lambda p: jnp.concatenate([jnp.where(strict, p[0:CHUNK], 0.0),
                                           jnp.where(incl, p[CHUNK:], 0.0)], axis=0).astype(BF16), ak)
    akv = each(lambda p, q: _dot(p, bd(q)), a_kk, v_c)
    n0 = each(lambda p: jnp.where(lvl0, p[0:CHUNK], 0.0), ab)
    pw = each(lambda p: mm(p, p), n0)
    x = each(lambda p: eye + p, n0)
    for _ in range(2):
        both = each(lambda p, q: mm(jnp.concatenate([p, q], axis=0), q), x, pw)
        x = each(lambda p, q: p + q[0:CHUNK], x, both)
        pw = each(lambda q: q[CHUNK:], both)
    x = each(lambda p, q: p + mm(p, q), x, pw)
    for lvl in (lvl1, lvl2):
        y = each(lambda p, q: mm(p, jnp.where(lvl, q[0:CHUNK], 0.0)), x, ab)
        x = each(lambda p, q: p + mm(q, p), x, y)
    tinv = each(lambda p: p.astype(BF16), x)
    wu = each(lambda p, a_, k_: _dot(p, jnp.concatenate([bd(a_), bd(k_[0:CHUNK].astype(BF16))], axis=1))
              .astype(BF16), tinv, at_c, akv)
    qo = each(lambda a_, w_: _dot(a_, jnp.concatenate([bd(w_[:, 0:GROUP_LANES]), bd(w_[:, GROUP_LANES:])],
                                                      axis=1)), a_rb, wu)
    q_c = each(lambda r_, p: (r_.astype(F32) + p[:, 0:GROUP_LANES]).astype(BF16), rt_c, qo)
    o_c = each(lambda p, k_: p[:, GROUP_LANES:] + k_[CHUNK:], qo, akv)
    zeros_c = jnp.zeros((CHUNK, GROUP_LANES), BF16)
    mc = each(lambda b_, w_, v_: _dot_tn(b_, jnp.concatenate(
        [w_, jnp.concatenate([zeros_c, v_], axis=1)], axis=0)), bk_c, wu, v_c)

    edges = [c * CHUNK if reverse else (c + 1) * CHUNK - 1 for c in range(n_chunks)]
    pad_rows = -n_chunks % 8
    edge_rows = jnp.concatenate([e_incl[e:e + 1, :] for e in edges]
                                + ([jnp.zeros((pad_rows, width), F32)] if pad_rows else []), axis=0)
    edge_cols = edge_rows.T
    p_c = {(c, gi): edge_cols[gi * GROUP_LANES:(gi + 1) * GROUP_LANES, c:c + 1] for c, gi in chains}
    m_c = each(lambda p, d: (bdf(p[:, 0:GROUP_LANES]) * d).astype(BF16), mc, p_c)
    c_c = each(lambda p, d: bdf(p[:, GROUP_LANES:]) * d, mc, p_c)

    st = [state_ref[gi] for gi in range(n_groups)]
    for c in order:
        for gi in range(n_groups):
            ch = (c, gi)
            both = _dot(jnp.concatenate([m_c[ch], q_c[ch]], axis=0), st[gi].astype(BF16))
            o_ref[0, c * CHUNK:(c + 1) * CHUNK, gi * GROUP_LANES:(gi + 1) * GROUP_LANES] = (
                both[GROUP_LANES:] + o_c[ch]).astype(o_ref.dtype)
            st[gi] = st[gi] * p_c[ch] + both[0:GROUP_LANES] + c_c[ch]
    for gi in range(n_groups):
        state_ref[gi] = st[gi]


def _rwkv_call(us3, wdi, wgate, w0, a0, k_k, k_a, r_k, ones_blocks, tri, *, reverse, width):
    bsz, s, cols = us3.shape
    tb = SCAN_BLOCK
    nt = s // tb

    def blk(t):
        return (nt - 1 - t) if reverse else t

    kern = functools.partial(_rwkv_kernel, reverse=reverse, width=width)
    const2 = lambda b, t: (0, 0)
    n_out = 2 if reverse else 3
    out_sds = jax.ShapeDtypeStruct((bsz, s, width), BF16)
    out_spec = pl.BlockSpec((1, tb, width), lambda b, t: (b, blk(t), 0))
    return pl.pallas_call(
        kern,
        grid=(bsz, nt),
        in_specs=[pl.BlockSpec((1, tb, cols), lambda b, t: (b, blk(t), 0)),
                  pl.BlockSpec(wdi.shape, const2),
                  pl.BlockSpec(wgate.shape, const2),
                  pl.BlockSpec((1, width), const2),
                  pl.BlockSpec((1, width), const2),
                  pl.BlockSpec((1, width), const2),
                  pl.BlockSpec((1, width), const2),
                  pl.BlockSpec((1, width), const2),
                  pl.BlockSpec(ones_blocks.shape, const2),
                  pl.BlockSpec(tri.shape, const2)],
        out_specs=[out_spec] * n_out,
        out_shape=[out_sds] * n_out,
        scratch_shapes=[pltpu.VMEM((width // GROUP_LANES, GROUP_LANES, GROUP_LANES), F32)],
        compiler_params=_cparams(("parallel", "arbitrary")),
        name="rwkv_bwd" if reverse else "rwkv_fwd",
    )(us3, wdi, wgate, w0, a0, k_k, k_a, r_k, ones_blocks, tri)


def _na_kernel(q_ref, k_ref, v_ref, bias_ref, o_ref, *, rows):
    g = pl.program_id(1)
    band = NA_BAND_ROWS * GRID_W
    start_row = jnp.clip(g * NA_ROWS_PER_STEP - NA_KH // 2, 0, rows - NA_BAND_ROWS)
    start = pl.multiple_of(start_row * GRID_W, GRID_W)
    n_pairs = q_ref.shape[2] // LANES
    lane = lax.broadcasted_iota(jnp.int32, (band, LANES), 1)
    for p in range(n_pairs):
        ls = slice(p * LANES, (p + 1) * LANES)
        qp = q_ref[0, :, ls]
        kp = k_ref[0, pl.ds(start, band), ls]
        vp = v_ref[0, pl.ds(start, band), ls]
        acc = None
        for hh in range(2):
            sel = (lane < HEAD_DIM) if hh == 0 else (lane >= HEAD_DIM)
            km = jnp.where(sel, kp, jnp.zeros_like(kp))
            vm = jnp.where(sel, vp, jnp.zeros_like(vp))
            sc = _dot_nt(qp, km) + bias_ref[0, 2 * p + hh]
            m = jnp.max(sc, axis=-1, keepdims=True)
            e = jnp.exp2(sc - m)
            inv = 1.0 / jnp.sum(e, axis=-1, keepdims=True)
            part = _dot(e.astype(BF16), vm) * inv
            acc = part if acc is None else acc + part
        o_ref[0, :, ls] = acc.astype(o_ref.dtype)


def _na_call(qkv3, bias, width):
    bsz, s, _ = qkv3.shape
    rows = s // GRID_W
    tq = NA_ROWS_PER_STEP * GRID_W
    ng = rows // NA_ROWS_PER_STEP
    nh = width // HEAD_DIM

    def bias_idx(b, g):
        ty = jnp.where(g == 0, 0, jnp.where(g == ng - 1, 2, 1))
        return (ty, 0, 0, 0)

    return pl.pallas_call(
        functools.partial(_na_kernel, rows=rows),
        grid=(bsz, ng),
        in_specs=[pl.BlockSpec((1, tq, width), lambda b, g: (b, g, 0)),
                  pl.BlockSpec((1, s, width), lambda b, g: (b, 0, 1)),
                  pl.BlockSpec((1, s, width), lambda b, g: (b, 0, 2)),
                  pl.BlockSpec((1, nh, tq, NA_BAND_ROWS * GRID_W), bias_idx)],
        out_specs=pl.BlockSpec((1, tq, width), lambda b, g: (b, g, 0)),
        out_shape=jax.ShapeDtypeStruct((bsz, s, width), BF16),
        compiler_params=_cparams(("parallel", "arbitrary")),
        name="nbr_attn",
    )(qkv3, qkv3, qkv3, bias)


def _na_bias_tables(rpb, rows):
    ng = rows // NA_ROWS_PER_STEP
    kh = min(NA_KH, rows)
    nh, n_dr, n_dc = rpb.shape
    col = np.arange(GRID_W)
    cs = np.clip(col - NA_KW // 2, 0, GRID_W - NA_KW)
    ok_c = (col[None, :] >= cs[:, None]) & (col[None, :] < cs[:, None] + NA_KW)
    dc = col[None, :] - col[:, None] + (NA_KW - 1)
    onehot = (dc[None] == np.arange(n_dc)[:, None, None]) & ok_c[None]
    expand = jnp.asarray(onehot.reshape(n_dc, GRID_W * GRID_W), F32)
    by_dr = jnp.dot(rpb.reshape(nh * n_dr, n_dc), expand, precision=lax.Precision.HIGHEST)
    by_dr = jnp.where(ok_c, by_dr.reshape(nh, n_dr, GRID_W, GRID_W), NEG_BIG)
    outside = jnp.full((nh, GRID_W, GRID_W), NEG_BIG, F32)
    tables = []
    for g in (0, 1, ng - 1):
        bs = int(np.clip(g * NA_ROWS_PER_STEP - NA_KH // 2, 0, rows - NA_BAND_ROWS))
        q_rows = []
        for qi in range(g * NA_ROWS_PER_STEP, (g + 1) * NA_ROWS_PER_STEP):
            rs = int(np.clip(qi - kh // 2, 0, rows - kh))
            blocks = [by_dr[:, kr - qi + NA_KH - 1] if rs <= kr < rs + kh else outside
                      for kr in range(bs, bs + NA_BAND_ROWS)]
            q_rows.append(jnp.concatenate(blocks, axis=-1))
        tables.append(jnp.concatenate(q_rows, axis=1))
    return jnp.stack(tables) * LOG2E


def _merge_kernel(o0_ref, o1_ref, b0_ref, b1_ref, g_ref, yb_ref, gate_ref, x_ref, ones_ref,
                  gng_ref, gnb_ref, wa_ref, wb_ref, wo_ref, lg_ref, lb_ref, xo_ref, xob_ref, *,
                  alpha, d_model):
    ones_bd = ones_ref[...]
    inv_n = 1.0 / HEAD_DIM
    o = o0_ref[...].astype(F32) + o1_ref[...].astype(F32)
    mean = _dot(o.astype(BF16), ones_bd) * inv_n
    oc = o - mean
    var = _dot((oc * oc).astype(BF16), ones_bd) * inv_n
    on = oc * lax.rsqrt(var + GN_EPS) * gng_ref[...] + gnb_ref[...]
    ya = (on + b0_ref[...].astype(F32) + b1_ref[...].astype(F32)) * g_ref[...].astype(F32)
    gates = gate_ref[...].astype(F32)
    ga = _sigmoid(gates[:, 0:d_model])
    gb = _sigmoid(gates[:, d_model:])
    merged = ga * _dot(ya.astype(BF16), wa_ref[...]) + gb * _dot(yb_ref[...], wb_ref[...])
    y = alpha * x_ref[...] + _dot(merged.astype(BF16), wo_ref[...])
    xn = _layer_norm_rows(y, lg_ref[...], lb_ref[...])
    xo_ref[...] = xn
    xob_ref[...] = xn.astype(BF16)


def _merge_call(o0, o1, b0, b1, g, yb, gates, x, ones_bd, gn_g, gn_b, wa, wb, wo, ln_g, ln_b,
                alpha, tm):
    m, d = x.shape
    w = o0.shape[1]
    rowblk = lambda n: pl.BlockSpec((tm, n), lambda i: (i, 0))
    const = lambda shape: pl.BlockSpec(shape, lambda i: (0, 0))
    return pl.pallas_call(
        functools.partial(_merge_kernel, alpha=alpha, d_model=d),
        grid=(m // tm,),
        in_specs=[rowblk(w), rowblk(w), rowblk(w), rowblk(w), rowblk(w), rowblk(w),
                  rowblk(2 * d), rowblk(d), const(ones_bd.shape),
                  const((1, w)), const((1, w)), const(wa.shape), const(wb.shape), const(wo.shape),
                  const((1, d)), const((1, d))],
        out_specs=[rowblk(d), rowblk(d)],
        out_shape=[jax.ShapeDtypeStruct((m, d), F32), jax.ShapeDtypeStruct((m, d), BF16)],
        compiler_params=_cparams(("parallel",)),
        name="merge_out_ln",
    )(o0, o1, b0, b1, g, yb, gates, x, ones_bd, gn_g, gn_b, wa, wb, wo, ln_g, ln_b)


def _ffn_kernel(x_ref, xb_ref, wg_ref, wu_ref, wo_ref, lg_ref, lb_ref, xo_ref, xob_ref, *, alpha):
    xb = xb_ref[...]
    hg = _dot(xb, wg_ref[...])
    hu = _dot(xb, wu_ref[...])
    act = (hg * _sigmoid(hg) * hu).astype(BF16)
    y = alpha * x_ref[...] + _dot(act, wo_ref[...])
    xn = _layer_norm_rows(y, lg_ref[...], lb_ref[...])
    xo_ref[...] = xn
    xob_ref[...] = xn.astype(BF16)


def _ffn_call(x, xb, wg, wu, wo, ln_g, ln_b, alpha, tm):
    m, d = x.shape
    rowblk = pl.BlockSpec((tm, d), lambda i: (i, 0))
    const = lambda shape: pl.BlockSpec(shape, lambda i: (0, 0), pipeline_mode=pl.Buffered(1))
    return pl.pallas_call(
        functools.partial(_ffn_kernel, alpha=alpha),
        grid=(m // tm,),
        in_specs=[rowblk, rowblk, const(wg.shape), const(wu.shape), const(wo.shape),
                  pl.BlockSpec((1, d), lambda i: (0, 0)), pl.BlockSpec((1, d), lambda i: (0, 0))],
        out_specs=[rowblk, rowblk],
        out_shape=[jax.ShapeDtypeStruct((m, d), F32), jax.ShapeDtypeStruct((m, d), BF16)],
        compiler_params=_cparams(("parallel",)),
        name="swiglu_ln",
    )(x, xb, wg, wu, wo, ln_g, ln_b)


def _low_rank_weights(decay_up_d, iclr_up_d, gate_up, direction, width):
    r_dec, r_icl, r_gate = decay_up_d.shape[0], iclr_up_d.shape[0], gate_up.shape[0]
    w = jnp.zeros((LANES, 2 * width), F32)
    w = w.at[direction * r_dec:(direction + 1) * r_dec, 0:width].set(decay_up_d)
    w = w.at[64 + direction * r_icl:64 + (direction + 1) * r_icl, width:].set(iclr_up_d)
    wg = jnp.zeros((LANES, width), F32).at[0:r_gate].set(gate_up)
    return w.astype(BF16), wg.astype(BF16)


def kernel(x, ln_in_g, ln_in_b, w_in, shift_mu, decay_w0, decay_up, iclr_a0, iclr_up, gate_up,
           k_k, k_a, r_k, gn_g, gn_b, na_rpb, w_branch_rwkv, w_branch_na, w_out, ln1_g, ln1_b,
           w_ffn_in, w_ffn_out, ln2_g, ln2_b):
    bsz, s, d = x.shape
    depth = w_in.shape[0]
    width = k_k.shape[1]
    na_width = w_branch_na.shape[1]
    d_ff = w_ffn_out.shape[1]
    rwkv_cols = shift_mu.shape[2]
    rwkv_pad = 3 * width + 2 * LANES
    assert 2 * decay_up.shape[2] == 64 and 2 * iclr_up.shape[2] == 64 and gate_up.shape[1] <= LANES
    assert rwkv_cols == 3 * width + LANES + gate_up.shape[1] and s % SCAN_BLOCK == 0
    rows = s // GRID_W
    assert rows % NA_ROWS_PER_STEP == 0 and rows >= NA_BAND_ROWS and rows // NA_ROWS_PER_STEP >= 3
    alpha = (2.0 * depth) ** 0.25
    m = bsz * s
    tm = 512 if m % 512 == 0 else 256

    ones_bd = jnp.asarray(np.kron(np.eye(width // HEAD_DIM), np.ones((HEAD_DIM, HEAD_DIM))), BF16)
    ones_blocks = ones_bd[0:MXU_WIDTH, 0:MXU_WIDTH]
    pos = np.arange(SCAN_BLOCK)
    same_chunk = (pos[:, None] // CHUNK) == (pos[None, :] // CHUNK)
    tri_f = jnp.asarray(same_chunk & (pos[None, :] <= pos[:, None]), BF16)
    tri_r = jnp.asarray(same_chunk & (pos[None, :] >= pos[:, None]), BF16)

    x2, xb = _ln_call(x.reshape(m, d), ln_in_g, ln_in_b, tm)
    row = lambda p: p.reshape(1, -1)
    for l in range(depth):
        wl = w_in[l]
        w_r = jnp.pad(wl[:, :rwkv_cols], ((0, 0), (0, rwkv_pad - rwkv_cols))).astype(BF16)
        w_n = wl[:, rwkv_cols:rwkv_cols + 3 * na_width]
        w_n = jnp.concatenate([w_n[:, :na_width] * (HEAD_DIM ** -0.5 * LOG2E), w_n[:, na_width:]],
                              axis=1).astype(BF16)
        w_g = wl[:, rwkv_cols + 3 * na_width:].astype(BF16)
        mu = jnp.pad(shift_mu[l], ((0, 0), (0, rwkv_pad - rwkv_cols)))

        us, qkv, gates = _proj_call(xb, w_r, w_n, w_g, mu, tm, s)
        us3 = us.reshape(bsz, s, rwkv_pad)

        outs = []
        for direction, tri in ((0, tri_f), (1, tri_r)):
            wdi, wgate = _low_rank_weights(decay_up[l, direction], iclr_up[l, direction], gate_up[l],
                                           direction, width)
            outs.append(_rwkv_call(
                us3, wdi, wgate, row(decay_w0[l, direction]), row(iclr_a0[l, direction]),
                row(k_k[l]), row(k_a[l]), row(r_k[l]), ones_blocks, tri,
                reverse=bool(direction), width=width))
        (o0, b0, g0), (o1, b1) = outs

        bias = _na_bias_tables(na_rpb[l], rows)
        yb = _na_call(qkv.reshape(bsz, s, 3 * na_width), bias, na_width)

        flat = lambda t: t.reshape(m, -1)
        x2, xb = _merge_call(
            flat(o0), flat(o1), flat(b0), flat(b1), flat(g0), flat(yb), gates, x2, ones_bd,
            row(gn_g[l]), row(gn_b[l]), w_branch_rwkv[l].astype(BF16), w_branch_na[l].astype(BF16),
            w_out[l].astype(BF16), row(ln1_g[l]), row(ln1_b[l]), alpha, tm)

        x2, xb = _ffn_call(
            x2, xb, w_ffn_in[l, :, :d_ff].astype(BF16), w_ffn_in[l, :, d_ff:].astype(BF16),
            w_ffn_out[l].astype(BF16), row(ln2_g[l]), row(ln2_b[l]), alpha, tm)
    return x2.reshape(bsz, s, d)
```

```python
import functools
import math

import jax
import jax.numpy as jnp
import numpy as np
from jax import lax
from jax.experimental import pallas as pl
from jax.experimental.pallas import tpu as pltpu

F32 = jnp.float32
BF16 = jnp.bfloat16

HEAD_DIM = 64
GRID_W = 64
NA_KH = 8
NA_KW = 16
NA_ROWS_PER_STEP = 4
NA_BAND_ROWS = NA_KH + NA_ROWS_PER_STEP - 1
LN_EPS = 1e-5
GN_EPS = 64e-5
NEG_BIG = -1e30
LOG2E = math.log2(math.e)
LANES = 128
MXU_WIDTH = 256
CHUNK = 64
SCAN_BLOCK = 256
SCAN_BATCH_ROWS = 2
HEADS_PER_GROUP = 2
GROUP_LANES = HEADS_PER_GROUP * HEAD_DIM
VMEM_LIMIT = 56 * 1024 * 1024


def _cparams(sem):
    return pltpu.CompilerParams(dimension_semantics=sem, vmem_limit_bytes=VMEM_LIMIT)


def _dot(a, b):
    return jnp.dot(a, b, preferred_element_type=F32)


def _dot_nt(a, b):
    return lax.dot_general(a, b, (((1,), (1,)), ((), ())), preferred_element_type=F32)


def _dot_tn(a, b):
    return lax.dot_general(a, b, (((0,), (0,)), ((), ())), preferred_element_type=F32)


def _layer_norm_rows(x, g, b):
    mean = jnp.mean(x, axis=-1, keepdims=True)
    xc = x - mean
    var = jnp.mean(xc * xc, axis=-1, keepdims=True)
    return xc * lax.rsqrt(var + LN_EPS) * g + b


def _sigmoid(x):
    return 1.0 / (1.0 + jnp.exp(-x))


def _ln_kernel(x_ref, g_ref, b_ref, o_ref, ob_ref):
    y = _layer_norm_rows(x_ref[...], g_ref[...], b_ref[...])
    o_ref[...] = y
    ob_ref[...] = y.astype(BF16)


def _ln_call(x, g, b, tm):
    m, d = x.shape
    return pl.pallas_call(
        _ln_kernel,
        grid=(m // tm,),
        in_specs=[pl.BlockSpec((tm, d), lambda i: (i, 0)),
                  pl.BlockSpec((1, d), lambda i: (0, 0)),
                  pl.BlockSpec((1, d), lambda i: (0, 0))],
        out_specs=[pl.BlockSpec((tm, d), lambda i: (i, 0)),
                   pl.BlockSpec((tm, d), lambda i: (i, 0))],
        out_shape=[jax.ShapeDtypeStruct((m, d), F32), jax.ShapeDtypeStruct((m, d), BF16)],
        compiler_params=_cparams(("parallel",)),
        name="input_ln",
    )(x, g.reshape(1, d), b.reshape(1, d))


def _proj_kernel(x_ref, xp_ref, xn_ref, wr_ref, wn_ref, wg_ref, mu_ref, us_ref, qkv_ref, gate_ref, *,
                 blocks_per_seq):
    i = pl.program_id(0)
    x = x_ref[...]
    tm = x.shape[0]
    qkv_ref[...] = _dot(x, wn_ref[...]).astype(BF16)
    gate_ref[...] = _dot(x, wg_ref[...]).astype(BF16)
    h = xp_ref.shape[0]
    ext = _dot(jnp.concatenate([xp_ref[...], x, xn_ref[...]], axis=0), wr_ref[...])
    u = ext[h:h + tm]
    row = lax.broadcasted_iota(jnp.int32, u.shape, 0)
    first = jnp.logical_and(row == 0, i % blocks_per_seq == 0)
    last = jnp.logical_and(row == tm - 1, i % blocks_per_seq == blocks_per_seq - 1)
    u_prev = jnp.where(first, 0.0, ext[h - 1:h - 1 + tm])
    u_next = jnp.where(last, 0.0, ext[h + 1:h + 1 + tm])
    us_ref[...] = u + mu_ref[0:1, :] * (u_prev - u) + mu_ref[1:2, :] * (u_next - u)


def _proj_call(xb, w_rwkv, w_na, w_gate, mu, tm, seq):
    m, d = xb.shape
    nr, nn, ng = w_rwkv.shape[1], w_na.shape[1], w_gate.shape[1]
    const = lambda i: (0, 0)
    halo = 16
    hb = tm // halo
    last = m // halo - 1
    return pl.pallas_call(
        functools.partial(_proj_kernel, blocks_per_seq=seq // tm),
        grid=(m // tm,),
        in_specs=[pl.BlockSpec((tm, d), lambda i: (i, 0)),
                  pl.BlockSpec((halo, d), lambda i: (jnp.maximum(i * hb - 1, 0), 0)),
                  pl.BlockSpec((halo, d), lambda i: (jnp.minimum((i + 1) * hb, last), 0)),
                  pl.BlockSpec((d, nr), const),
                  pl.BlockSpec((d, nn), const),
                  pl.BlockSpec((d, ng), const),
                  pl.BlockSpec(mu.shape, const)],
        out_specs=[pl.BlockSpec((tm, nr), lambda i: (i, 0)),
                   pl.BlockSpec((tm, nn), lambda i: (i, 0)),
                   pl.BlockSpec((tm, ng), lambda i: (i, 0))],
        out_shape=[jax.ShapeDtypeStruct((m, nr), F32),
                   jax.ShapeDtypeStruct((m, nn), BF16),
                   jax.ShapeDtypeStruct((m, ng), BF16)],
        compiler_params=_cparams(("parallel",)),
        name="in_proj",
    )(xb, xb, xb, w_rwkv, w_na, w_gate, mu)


def _block_diag(x, bd_mask):
    tiled = jnp.concatenate([x] * HEADS_PER_GROUP, axis=0)
    return jnp.where(bd_mask, tiled, jnp.zeros_like(tiled))


def _head_sums(xs, ones_blocks):
    rows, width = xs[0].shape
    gw = ones_blocks.shape[0]
    n_g = width // gw
    stacked = jnp.concatenate([x[:, i * gw:(i + 1) * gw] for x in xs for i in range(n_g)], axis=0)
    res = _dot(stacked.astype(BF16), ones_blocks)
    return [jnp.concatenate([res[(j * n_g + i) * rows:(j * n_g + i + 1) * rows] for i in range(n_g)],
                            axis=1) for j in range(len(xs))]


def _rwkv_kernel(us_ref, wdi_ref, wgate_ref, w0_ref, a0_ref, kk_ref, ka_ref, rk_ref, ones_ref, tri_ref,
                 o_ref, bonus_ref, *rest, reverse, width):
    g_ref = None if reverse else rest[0]
    state_ref = rest[-1]
    t = pl.program_id(1)
    bps, tb = us_ref.shape[0], us_ref.shape[1]
    rows_all = bps * tb
    c3 = 3 * width

    @pl.when(t == 0)
    def _():
        state_ref[...] = jnp.zeros_like(state_ref)

    us = us_ref[...].reshape(rows_all, us_ref.shape[2])
    r = us[:, 0:width]
    k = us[:, width:2 * width]
    v = us[:, 2 * width:c3]
    low = us[:, c3:c3 + 2 * LANES]

    lane = lax.broadcasted_iota(jnp.int32, (rows_all, LANES), 1)
    f_di = jnp.where(lane < 64, jnp.tanh(low[:, 0:LANES]), low[:, 0:LANES]).astype(BF16)
    lr = _dot(f_di, wdi_ref[...])
    z = w0_ref[...] + lr[:, 0:width]
    iclr = _sigmoid(a0_ref[...] + lr[:, width:2 * width])
    if g_ref is not None:
        g_ref[...] = (_dot(_sigmoid(low[:, LANES:]).astype(BF16), wgate_ref[...])
                      .astype(g_ref.dtype).reshape(g_ref.shape))

    logw = -math.exp(-0.5) * _sigmoid(z)

    kkr = k * kk_ref[...]
    kd = k * (1.0 + (iclr - 1.0) * ka_ref[...])
    ss, rkd = _head_sums([kkr * kkr, r * kd * rk_ref[...]], ones_ref[...])
    kk = kkr * lax.rsqrt(jnp.maximum(ss, 1e-24))
    a = -kk
    b = kk * iclr
    bonus_ref[...] = (rkd * v).astype(bonus_ref.dtype).reshape(bonus_ref.shape)

    hi = logw.astype(BF16)
    lo = (logw - hi.astype(F32)).astype(BF16)
    tri = tri_ref[...]
    cum = jnp.concatenate([_dot(tri, hi[bb * tb:(bb + 1) * tb]) + _dot(tri, lo[bb * tb:(bb + 1) * tb])
                           for bb in range(bps)], axis=0)

    e_incl = jnp.exp(cum)
    a_t = (a * jnp.exp(cum - logw)).astype(BF16)
    r_t = (r * e_incl).astype(BF16)
    e_neg = jnp.exp(-cum)
    b_t = (b * e_neg).astype(BF16)
    k_t = (kd * e_neg).astype(BF16)
    v_b = v.astype(BF16)

    shape_c = (CHUNK, GROUP_LANES)
    ti = lax.broadcasted_iota(jnp.int32, shape_c, 0)
    ji = lax.broadcasted_iota(jnp.int32, shape_c, 1) % CHUNK
    strict = (ji > ti) if reverse else (ji < ti)
    incl = (ji >= ti) if reverse else (ji <= ti)
    eye = jnp.where(ji == ti, 1.0, 0.0).astype(F32)
    same16 = (ti // 16) == (ji // 16)
    same32 = (ti // 32) == (ji // 32)
    lvl0 = strict & same16
    lvl1 = strict & same32 & jnp.logical_not(same16)
    lvl2 = strict & jnp.logical_not(same32)
    bi = lax.broadcasted_iota(jnp.int32, (GROUP_LANES, GROUP_LANES), 0) // HEAD_DIM
    bj = lax.broadcasted_iota(jnp.int32, (GROUP_LANES, GROUP_LANES), 1) // HEAD_DIM
    bd_mask = bi == bj

    def mm(x, y):
        return _dot(x.astype(BF16), _block_diag(y.astype(BF16), bd_mask))

    n_chunks = tb // CHUNK
    n_groups = width // GROUP_LANES
    order = range(n_chunks - 1, -1, -1) if reverse else range(n_chunks)
    chains = [(bb, c, gi) for bb in range(bps) for c in range(n_chunks) for gi in range(n_groups)]

    def part(arr, ch):
        bb, c, gi = ch
        r0 = bb * tb + c * CHUNK
        return arr[r0:r0 + CHUNK, gi * GROUP_LANES:(gi + 1) * GROUP_LANES]

    def each(fn, *dicts):
        return {ch: fn(*(d[ch] for d in dicts)) for ch in chains}

    bd = lambda y: _block_diag(y, bd_mask)
    bdf = lambda y: jnp.where(bd_mask, y, 0.0)

    at_c = {ch: part(a_t, ch) for ch in chains}
    rt_c = {ch: part(r_t, ch) for ch in chains}
    bk_c = {ch: jnp.concatenate([part(b_t, ch), part(k_t, ch)], axis=0) for ch in chains}
    v_c = {ch: part(v_b, ch) for ch in chains}
    ar = each(lambda p, q: jnp.concatenate([p, q], axis=0), at_c, rt_c)
    ab = each(lambda p, q: _dot_nt(p, bd(q[0:CHUNK])), ar, bk_c)
    ak = each(lambda p, q: _dot_nt(p, bd(q[CHUNK:])), ar, bk_c)
    a_rb = each(lambda p: jnp.where(incl, p[CHUNK:], 0.0).astype(BF16), ab)
    a_kk = each(lambda p: jnp.concatenate([jnp.where(strict, p[0:CHUNK], 0.0),
                                           jnp.where(incl, p[CHUNK:], 0.0)], axis=0).astype(BF16), ak)
    akv = each(lambda p, q: _dot(p, bd(q)), a_kk, v_c)
    n0 = each(lambda p: jnp.where(lvl0, p[0:CHUNK], 0.0), ab)
    pw = each(lambda p: mm(p, p), n0)
    x = each(lambda p: eye + p, n0)
    for _ in range(2):
        both = each(lambda p, q: mm(jnp.concatenate([p, q], axis=0), q), x, pw)
        x = each(lambda p, q: p + q[0:CHUNK], x, both)
        pw = each(lambda q: q[CHUNK:], both)
    x = each(lambda p, q: p + mm(p, q), x, pw)
    for lvl in (lvl1, lvl2):
        y = each(lambda p, q: mm(p, jnp.where(lvl, q[0:CHUNK], 0.0)), x, ab)
        x = each(lambda p, q: p + mm(q, p), x, y)
    tinv = each(lambda p: p.astype(BF16), x)
    wu = each(lambda p, a_, k_: _dot(p, jnp.concatenate([bd(a_), bd(k_[0:CHUNK].astype(BF16))], axis=1))
              .astype(BF16), tinv, at_c, akv)
    qo = each(lambda a_, w_: _dot(a_, jnp.concatenate([bd(w_[:, 0:GROUP_LANES]), bd(w_[:, GROUP_LANES:])],
                                                      axis=1)), a_rb, wu)
    q_c = each(lambda r_, p: (r_.astype(F32) + p[:, 0:GROUP_LANES]).astype(BF16), rt_c, qo)
    o_c = each(lambda p, k_: p[:, GROUP_LANES:] + k_[CHUNK:], qo, akv)
    zeros_c = jnp.zeros((CHUNK, GROUP_LANES), BF16)
    mc = each(lambda b_, w_, v_: _dot_tn(b_, jnp.concatenate(
        [w_, jnp.concatenate([zeros_c, v_], axis=1)], axis=0)), bk_c, wu, v_c)

    edges = [bb * tb + (c * CHUNK if reverse else (c + 1) * CHUNK - 1)
             for bb in range(bps) for c in range(n_chunks)]
    pad_rows = -len(edges) % 8
    edge_rows = jnp.concatenate([e_incl[e:e + 1, :] for e in edges]
                                + ([jnp.zeros((pad_rows, width), F32)] if pad_rows else []), axis=0)
    edge_cols = edge_rows.T
    p_c = {(bb, c, gi): edge_cols[gi * GROUP_LANES:(gi + 1) * GROUP_LANES,
                                  bb * n_chunks + c:bb * n_chunks + c + 1] for bb, c, gi in chains}
    m_c = each(lambda p, d: (bdf(p[:, 0:GROUP_LANES]) * d).astype(BF16), mc, p_c)
    c_c = each(lambda p, d: bdf(p[:, GROUP_LANES:]) * d, mc, p_c)

    lines = [(bb, gi) for bb in range(bps) for gi in range(n_groups)]
    st = {ln: state_ref[ln[0] * n_groups + ln[1]] for ln in lines}
    for c in order:
        for bb, gi in lines:
            ch = (bb, c, gi)
            both = _dot(jnp.concatenate([m_c[ch], q_c[ch]], axis=0), st[bb, gi].astype(BF16))
            o_ref[bb, c * CHUNK:(c + 1) * CHUNK, gi * GROUP_LANES:(gi + 1) * GROUP_LANES] = (
                both[GROUP_LANES:] + o_c[ch]).astype(o_ref.dtype)
            st[bb, gi] = st[bb, gi] * p_c[ch] + both[0:GROUP_LANES] + c_c[ch]
    for bb, gi in lines:
        state_ref[bb * n_groups + gi] = st[bb, gi]


def _rwkv_call(us3, wdi, wgate, w0, a0, k_k, k_a, r_k, ones_blocks, tri, *, reverse, width):
    bsz, s, cols = us3.shape
    tb = SCAN_BLOCK
    nt = s // tb

    def blk(t):
        return (nt - 1 - t) if reverse else t

    kern = functools.partial(_rwkv_kernel, reverse=reverse, width=width)
    const2 = lambda b, t: (0, 0)
    n_out = 2 if reverse else 3
    out_sds = jax.ShapeDtypeStruct((bsz, s, width), BF16)
    bps = SCAN_BATCH_ROWS if bsz % SCAN_BATCH_ROWS == 0 else 1
    out_spec = pl.BlockSpec((bps, tb, width), lambda b, t: (b, blk(t), 0))
    return pl.pallas_call(
        kern,
        grid=(bsz // bps, nt),
        in_specs=[pl.BlockSpec((bps, tb, cols), lambda b, t: (b, blk(t), 0)),
                  pl.BlockSpec(wdi.shape, const2),
                  pl.BlockSpec(wgate.shape, const2),
                  pl.BlockSpec((1, width), const2),
                  pl.BlockSpec((1, width), const2),
                  pl.BlockSpec((1, width), const2),
                  pl.BlockSpec((1, width), const2),
                  pl.BlockSpec((1, width), const2),
                  pl.BlockSpec(ones_blocks.shape, const2),
                  pl.BlockSpec(tri.shape, const2)],
        out_specs=[out_spec] * n_out,
        out_shape=[out_sds] * n_out,
        scratch_shapes=[pltpu.VMEM((bps * (width // GROUP_LANES), GROUP_LANES, GROUP_LANES), F32)],
        compiler_params=_cparams(("parallel", "arbitrary")),
        name="rwkv_bwd" if reverse else "rwkv_fwd",
    )(us3, wdi, wgate, w0, a0, k_k, k_a, r_k, ones_blocks, tri)


def _na_kernel(q_ref, k_ref, v_ref, bias_ref, o_ref, *, rows):
    g = pl.program_id(1)
    band = NA_BAND_ROWS * GRID_W
    start_row = jnp.clip(g * NA_ROWS_PER_STEP - NA_KH // 2, 0, rows - NA_BAND_ROWS)
    start = pl.multiple_of(start_row * GRID_W, GRID_W)
    n_pairs = q_ref.shape[2] // LANES
    tq = q_ref.shape[1]
    lane = lax.broadcasted_iota(jnp.int32, (band, LANES), 1)
    lane_q = lax.broadcasted_iota(jnp.int32, (tq, LANES), 1)
    for p in range(n_pairs):
        ls = slice(p * LANES, (p + 1) * LANES)
        qp = q_ref[0, :, ls]
        kp = k_ref[0, pl.ds(start, band), ls]
        vp = v_ref[0, pl.ds(start, band), ls]
        acc = None
        for hh in range(2):
            sel = (lane < HEAD_DIM) if hh == 0 else (lane >= HEAD_DIM)
            ones_lane = HEAD_DIM if hh == 0 else 0
            km = jnp.where(sel, kp, jnp.zeros_like(kp))
            vm = jnp.where(sel, vp, jnp.where(lane == ones_lane, 1.0, 0.0).astype(vp.dtype))
            sc = _dot_nt(qp, km) + bias_ref[0, 2 * p + hh]
            m = jnp.max(sc, axis=-1, keepdims=True)
            e = jnp.exp2((sc - m).astype(BF16))
            pv = _dot(e, vm)
            inv = 1.0 / pv[:, ones_lane:ones_lane + 1]
            part = jnp.where((lane_q < HEAD_DIM) == (hh == 0), pv * inv, 0.0)
            acc = part if acc is None else acc + part
        o_ref[0, :, ls] = acc.astype(o_ref.dtype)


def _na_call(qkv3, bias, width):
    bsz, s, _ = qkv3.shape
    rows = s // GRID_W
    tq = NA_ROWS_PER_STEP * GRID_W
    ng = rows // NA_ROWS_PER_STEP
    nh = width // HEAD_DIM

    def bias_idx(b, g):
        ty = jnp.where(g == 0, 0, jnp.where(g == ng - 1, 2, 1))
        return (ty, 0, 0, 0)

    return pl.pallas_call(
        functools.partial(_na_kernel, rows=rows),
        grid=(bsz, ng),
        in_specs=[pl.BlockSpec((1, tq, width), lambda b, g: (b, g, 0)),
                  pl.BlockSpec((1, s, width), lambda b, g: (b, 0, 1)),
                  pl.BlockSpec((1, s, width), lambda b, g: (b, 0, 2)),
                  pl.BlockSpec((1, nh, tq, NA_BAND_ROWS * GRID_W), bias_idx)],
        out_specs=pl.BlockSpec((1, tq, width), lambda b, g: (b, g, 0)),
        out_shape=jax.ShapeDtypeStruct((bsz, s, width), BF16),
        compiler_params=_cparams(("parallel", "arbitrary")),
        name="nbr_attn",
    )(qkv3, qkv3, qkv3, bias)


def _na_bias_tables(rpb, rows):
    ng = rows // NA_ROWS_PER_STEP
    kh = min(NA_KH, rows)
    nh, n_dr, n_dc = rpb.shape
    col = np.arange(GRID_W)
    cs = np.clip(col - NA_KW // 2, 0, GRID_W - NA_KW)
    ok_c = (col[None, :] >= cs[:, None]) & (col[None, :] < cs[:, None] + NA_KW)
    dc = col[None, :] - col[:, None] + (NA_KW - 1)
    onehot = (dc[None] == np.arange(n_dc)[:, None, None]) & ok_c[None]
    expand = jnp.asarray(onehot.reshape(n_dc, GRID_W * GRID_W), F32)
    by_dr = jnp.dot(rpb.reshape(nh * n_dr, n_dc), expand, precision=lax.Precision.HIGHEST)
    by_dr = jnp.where(ok_c, by_dr.reshape(nh, n_dr, GRID_W, GRID_W), NEG_BIG)
    outside = jnp.full((nh, GRID_W, GRID_W), NEG_BIG, F32)
    tables = []
    for g in (0, 1, ng - 1):
        bs = int(np.clip(g * NA_ROWS_PER_STEP - NA_KH // 2, 0, rows - NA_BAND_ROWS))
        q_rows = []
        for qi in range(g * NA_ROWS_PER_STEP, (g + 1) * NA_ROWS_PER_STEP):
            rs = int(np.clip(qi - kh // 2, 0, rows - kh))
            blocks = [by_dr[:, kr - qi + NA_KH - 1] if rs <= kr < rs + kh else outside
                      for kr in range(bs, bs + NA_BAND_ROWS)]
            q_rows.append(jnp.concatenate(blocks, axis=-1))
        tables.append(jnp.concatenate(q_rows, axis=1))
    return jnp.stack(tables) * LOG2E


def _merge_kernel(o0_ref, o1_ref, b0_ref, b1_ref, g_ref, yb_ref, gate_ref, x_ref, ones_ref,
                  gng_ref, gnb_ref, wa_ref, wb_ref, wo_ref, lg_ref, lb_ref, xo_ref, xob_ref, *,
                  alpha, d_model):
    ones_bd = ones_ref[...]
    inv_n = 1.0 / HEAD_DIM
    o = o0_ref[...].astype(F32) + o1_ref[...].astype(F32)
    mean = _dot(o.astype(BF16), ones_bd) * inv_n
    oc = o - mean
    var = _dot((oc * oc).astype(BF16), ones_bd) * inv_n
    on = oc * lax.rsqrt(var + GN_EPS) * gng_ref[...] + gnb_ref[...]
    ya = (on + b0_ref[...].astype(F32) + b1_ref[...].astype(F32)) * g_ref[...].astype(F32)
    gates = gate_ref[...].astype(F32)
    ga = _sigmoid(gates[:, 0:d_model])
    gb = _sigmoid(gates[:, d_model:])
    merged = ga * _dot(ya.astype(BF16), wa_ref[...]) + gb * _dot(yb_ref[...], wb_ref[...])
    y = alpha * x_ref[...] + _dot(merged.astype(BF16), wo_ref[...])
    xn = _layer_norm_rows(y, lg_ref[...], lb_ref[...])
    xo_ref[...] = xn
    xob_ref[...] = xn.astype(BF16)


def _merge_call(o0, o1, b0, b1, g, yb, gates, x, ones_bd, gn_g, gn_b, wa, wb, wo, ln_g, ln_b,
                alpha, tm):
    m, d = x.shape
    w = o0.shape[1]
    rowblk = lambda n: pl.BlockSpec((tm, n), lambda i: (i, 0))
    const = lambda shape: pl.BlockSpec(shape, lambda i: (0, 0))
    return pl.pallas_call(
        functools.partial(_merge_kernel, alpha=alpha, d_model=d),
        grid=(m // tm,),
        in_specs=[rowblk(w), rowblk(w), rowblk(w), rowblk(w), rowblk(w), rowblk(w),
                  rowblk(2 * d), rowblk(d), const(ones_bd.shape),
                  const((1, w)), const((1, w)), const(wa.shape), const(wb.shape), const(wo.shape),
                  const((1, d)), const((1, d))],
        out_specs=[rowblk(d), rowblk(d)],
        out_shape=[jax.ShapeDtypeStruct((m, d), F32), jax.ShapeDtypeStruct((m, d), BF16)],
        compiler_params=_cparams(("parallel",)),
        name="merge_out_ln",
    )(o0, o1, b0, b1, g, yb, gates, x, ones_bd, gn_g, gn_b, wa, wb, wo, ln_g, ln_b)


def _ffn_kernel(x_ref, xb_ref, wg_ref, wu_ref, wo_ref, lg_ref, lb_ref, xo_ref, xob_ref, *, alpha):
    xb = xb_ref[...]
    hg = _dot(xb, wg_ref[...])
    hu = _dot(xb, wu_ref[...])
    act = (hg * _sigmoid(hg) * hu).astype(BF16)
    y = alpha * x_ref[...] + _dot(act, wo_ref[...])
    xn = _layer_norm_rows(y, lg_ref[...], lb_ref[...])
    xo_ref[...] = xn
    xob_ref[...] = xn.astype(BF16)


def _ffn_call(x, xb, wg, wu, wo, ln_g, ln_b, alpha, tm):
    m, d = x.shape
    rowblk = pl.BlockSpec((tm, d), lambda i: (i, 0))
    const = lambda shape: pl.BlockSpec(shape, lambda i: (0, 0), pipeline_mode=pl.Buffered(1))
    return pl.pallas_call(
        functools.partial(_ffn_kernel, alpha=alpha),
        grid=(m // tm,),
        in_specs=[rowblk, rowblk, const(wg.shape), const(wu.shape), const(wo.shape),
                  pl.BlockSpec((1, d), lambda i: (0, 0)), pl.BlockSpec((1, d), lambda i: (0, 0))],
        out_specs=[rowblk, rowblk],
        out_shape=[jax.ShapeDtypeStruct((m, d), F32), jax.ShapeDtypeStruct((m, d), BF16)],
        compiler_params=_cparams(("parallel",)),
        name="swiglu_ln",
    )(x, xb, wg, wu, wo, ln_g, ln_b)


def _low_rank_weights(decay_up_d, iclr_up_d, gate_up, direction, width):
    r_dec, r_icl, r_gate = decay_up_d.shape[0], iclr_up_d.shape[0], gate_up.shape[0]
    w = jnp.zeros((LANES, 2 * width), F32)
    w = w.at[direction * r_dec:(direction + 1) * r_dec, 0:width].set(decay_up_d)
    w = w.at[64 + direction * r_icl:64 + (direction + 1) * r_icl, width:].set(iclr_up_d)
    wg = jnp.zeros((LANES, width), F32).at[0:r_gate].set(gate_up)
    return w.astype(BF16), wg.astype(BF16)


def kernel(x, ln_in_g, ln_in_b, w_in, shift_mu, decay_w0, decay_up, iclr_a0, iclr_up, gate_up,
           k_k, k_a, r_k, gn_g, gn_b, na_rpb, w_branch_rwkv, w_branch_na, w_out, ln1_g, ln1_b,
           w_ffn_in, w_ffn_out, ln2_g, ln2_b):
    bsz, s, d = x.shape
    depth = w_in.shape[0]
    width = k_k.shape[1]
    na_width = w_branch_na.shape[1]
    d_ff = w_ffn_out.shape[1]
    rwkv_cols = shift_mu.shape[2]
    rwkv_pad = 3 * width + 2 * LANES
    assert 2 * decay_up.shape[2] == 64 and 2 * iclr_up.shape[2] == 64 and gate_up.shape[1] <= LANES
    assert rwkv_cols == 3 * width + LANES + gate_up.shape[1] and s % SCAN_BLOCK == 0
    rows = s // GRID_W
    assert rows % NA_ROWS_PER_STEP == 0 and rows >= NA_BAND_ROWS and rows // NA_ROWS_PER_STEP >= 3
    alpha = (2.0 * depth) ** 0.25
    m = bsz * s
    tm = 512 if m % 512 == 0 else 256

    ones_bd = jnp.asarray(np.kron(np.eye(width // HEAD_DIM), np.ones((HEAD_DIM, HEAD_DIM))), BF16)
    ones_blocks = ones_bd[0:MXU_WIDTH, 0:MXU_WIDTH]
    pos = np.arange(SCAN_BLOCK)
    same_chunk = (pos[:, None] // CHUNK) == (pos[None, :] // CHUNK)
    tri_f = jnp.asarray(same_chunk & (pos[None, :] <= pos[:, None]), BF16)
    tri_r = jnp.asarray(same_chunk & (pos[None, :] >= pos[:, None]), BF16)

    x2, xb = _ln_call(x.reshape(m, d), ln_in_g, ln_in_b, tm)
    row = lambda p: p.reshape(1, -1)
    for l in range(depth):
        wl = w_in[l]
        w_r = jnp.pad(wl[:, :rwkv_cols], ((0, 0), (0, rwkv_pad - rwkv_cols))).astype(BF16)
        w_n = wl[:, rwkv_cols:rwkv_cols + 3 * na_width]
        w_n = jnp.concatenate([w_n[:, :na_width] * (HEAD_DIM ** -0.5 * LOG2E), w_n[:, na_width:]],
                              axis=1).astype(BF16)
        w_g = wl[:, rwkv_cols + 3 * na_width:].astype(BF16)
        mu = jnp.pad(shift_mu[l], ((0, 0), (0, rwkv_pad - rwkv_cols)))

        us, qkv, gates = _proj_call(xb, w_r, w_n, w_g, mu, tm, s)
        us3 = us.reshape(bsz, s, rwkv_pad)

        outs = []
        for direction, tri in ((0, tri_f), (1, tri_r)):
            wdi, wgate = _low_rank_weights(decay_up[l, direction], iclr_up[l, direction], gate_up[l],
                                           direction, width)
            outs.append(_rwkv_call(
                us3, wdi, wgate, row(decay_w0[l, direction]), row(iclr_a0[l, direction]),
                row(k_k[l]), row(k_a[l]), row(r_k[l]), ones_blocks, tri,
                reverse=bool(direction), width=width))
        (o0, b0, g0), (o1, b1) = outs

        bias = _na_bias_tables(na_rpb[l], rows)
        yb = _na_call(qkv.reshape(bsz, s, 3 * na_width), bias, na_width)

        flat = lambda t: t.reshape(m, -1)
        x2, xb = _merge_call(
            flat(o0), flat(o1), flat(b0), flat(b1), flat(g0), flat(yb), gates, x2, ones_bd,
            row(gn_g[l]), row(gn_b[l]), w_branch_rwkv[l].astype(BF16), w_branch_na[l].astype(BF16),
            w_out[l].astype(BF16), row(ln1_g[l]), row(ln1_b[l]), alpha, tm)

        x2, xb = _ffn_call(
            x2, xb, w_ffn_in[l, :, :d_ff].astype(BF16), w_ffn_in[l, :, d_ff:].astype(BF16),
            w_ffn_out[l].astype(BF16), row(ln2_g[l]), row(ln2_b[l]), alpha, tm)
    return x2.reshape(bsz, s, d)
```

```python
import functools
import math

import jax
import jax.numpy as jnp
import numpy as np
from jax import lax
from jax.experimental import pallas as pl
from jax.experimental.pallas import tpu as pltpu

F32 = jnp.float32
BF16 = jnp.bfloat16

HEAD_DIM = 64
GRID_W = 64
NA_KH = 8
NA_KW = 16
NA_ROWS_PER_STEP = 4
NA_BAND_ROWS = NA_KH + NA_ROWS_PER_STEP - 1
LN_EPS = 1e-5
GN_EPS = 64e-5
NEG_BIG = -1e30
LOG2E = math.log2(math.e)
LANES = 128
MXU_WIDTH = 256
CHUNK = 64
SCAN_BLOCK = 256
SCAN_BATCH_ROWS = 2
HEADS_PER_GROUP = 2
GROUP_LANES = HEADS_PER_GROUP * HEAD_DIM
ROW_SPLIT = 2
VMEM_LIMIT = 56 * 1024 * 1024


def _cparams(sem):
    return pltpu.CompilerParams(dimension_semantics=sem, vmem_limit_bytes=VMEM_LIMIT)


def _dot(a, b):
    return jnp.dot(a, b, preferred_element_type=F32)


def _dot_nt(a, b):
    return lax.dot_general(a, b, (((1,), (1,)), ((), ())), preferred_element_type=F32)


def _dot_tn(a, b):
    return lax.dot_general(a, b, (((0,), (0,)), ((), ())), preferred_element_type=F32)


def _layer_norm_rows(x, g, b):
    mean = jnp.mean(x, axis=-1, keepdims=True)
    xc = x - mean
    var = jnp.mean(xc * xc, axis=-1, keepdims=True)
    return xc * lax.rsqrt(var + LN_EPS) * g + b


def _sigmoid(x):
    return 1.0 / (1.0 + jnp.exp(-x))


def _ln_kernel(x_ref, g_ref, b_ref, o_ref, ob_ref):
    y = _layer_norm_rows(x_ref[...], g_ref[...], b_ref[...])
    o_ref[...] = y
    ob_ref[...] = y.astype(BF16)


def _ln_call(x, g, b, tm):
    m, d = x.shape
    return pl.pallas_call(
        _ln_kernel,
        grid=(m // tm,),
        in_specs=[pl.BlockSpec((tm, d), lambda i: (i, 0)),
                  pl.BlockSpec((1, d), lambda i: (0, 0)),
                  pl.BlockSpec((1, d), lambda i: (0, 0))],
        out_specs=[pl.BlockSpec((tm, d), lambda i: (i, 0)),
                   pl.BlockSpec((tm, d), lambda i: (i, 0))],
        out_shape=[jax.ShapeDtypeStruct((m, d), F32), jax.ShapeDtypeStruct((m, d), BF16)],
        compiler_params=_cparams(("parallel",)),
        name="input_ln",
    )(x, g.reshape(1, d), b.reshape(1, d))


def _proj_kernel(x_ref, xp_ref, xn_ref, wr_ref, wn_ref, wg_ref, mu_ref, us_ref, qkv_ref, gate_ref, *,
                 blocks_per_seq):
    i = pl.program_id(0)
    x = x_ref[...]
    tm = x.shape[0]
    h = xp_ref.shape[0]
    ext = _dot(jnp.concatenate([xp_ref[...], x, xn_ref[...]], axis=0), wr_ref[...])
    u = ext[h:h + tm]
    row = lax.broadcasted_iota(jnp.int32, u.shape, 0)
    first = jnp.logical_and(row == 0, i % blocks_per_seq == 0)
    last = jnp.logical_and(row == tm - 1, i % blocks_per_seq == blocks_per_seq - 1)
    u_prev = jnp.where(first, 0.0, ext[h - 1:h - 1 + tm])
    u_next = jnp.where(last, 0.0, ext[h + 1:h + 1 + tm])
    us_ref[...] = u + mu_ref[0:1, :] * (u_prev - u) + mu_ref[1:2, :] * (u_next - u)
    qkv_ref[...] = _dot(x, wn_ref[...]).astype(BF16)
    gate_ref[...] = _dot(x, wg_ref[...]).astype(BF16)


def _proj_call(xb, w_rwkv, w_na, w_gate, mu, tm, seq):
    m, d = xb.shape
    nr, nn, ng = w_rwkv.shape[1], w_na.shape[1], w_gate.shape[1]
    const = lambda i: (0, 0)
    halo = 16
    hb = tm // halo
    last = m // halo - 1
    return pl.pallas_call(
        functools.partial(_proj_kernel, blocks_per_seq=seq // tm),
        grid=(m // tm,),
        in_specs=[pl.BlockSpec((tm, d), lambda i: (i, 0)),
                  pl.BlockSpec((halo, d), lambda i: (jnp.maximum(i * hb - 1, 0), 0)),
                  pl.BlockSpec((halo, d), lambda i: (jnp.minimum((i + 1) * hb, last), 0)),
                  pl.BlockSpec((d, nr), const),
                  pl.BlockSpec((d, nn), const),
                  pl.BlockSpec((d, ng), const),
                  pl.BlockSpec(mu.shape, const)],
        out_specs=[pl.BlockSpec((tm, nr), lambda i: (i, 0)),
                   pl.BlockSpec((tm, nn), lambda i: (i, 0)),
                   pl.BlockSpec((tm, ng), lambda i: (i, 0))],
        out_shape=[jax.ShapeDtypeStruct((m, nr), F32),
                   jax.ShapeDtypeStruct((m, nn), BF16),
                   jax.ShapeDtypeStruct((m, ng), BF16)],
        compiler_params=_cparams(("parallel",)),
        name="in_proj",
    )(xb, xb, xb, w_rwkv, w_na, w_gate, mu)


def _block_diag(x, bd_mask):
    tiled = jnp.concatenate([x] * HEADS_PER_GROUP, axis=0)
    return jnp.where(bd_mask, tiled, jnp.zeros_like(tiled))


def _head_sums(xs, ones_blocks):
    rows, width = xs[0].shape
    gw = ones_blocks.shape[0]
    n_g = width // gw
    stacked = jnp.concatenate([x[:, i * gw:(i + 1) * gw] for x in xs for i in range(n_g)], axis=0)
    res = _dot(stacked.astype(BF16), ones_blocks)
    return [jnp.concatenate([res[(j * n_g + i) * rows:(j * n_g + i + 1) * rows] for i in range(n_g)],
                            axis=1) for j in range(len(xs))]


def _rwkv_kernel(us_ref, wdi_ref, wgate_ref, w0_ref, a0_ref, kk_ref, ka_ref, rk_ref, ones_ref, tri_ref,
                 o_ref, bonus_ref, *rest, reverse, width):
    g_ref = None if reverse else rest[0]
    state_ref = rest[-1]
    t = pl.program_id(1)
    bps, tb = us_ref.shape[0], us_ref.shape[1]
    rows_all = bps * tb
    c3 = 3 * width

    @pl.when(t == 0)
    def _():
        state_ref[...] = jnp.zeros_like(state_ref)

    us = us_ref[...].reshape(rows_all, us_ref.shape[2])
    r = us[:, 0:width]
    k = us[:, width:2 * width]
    v = us[:, 2 * width:c3]
    low = us[:, c3:c3 + 2 * LANES]

    lane = lax.broadcasted_iota(jnp.int32, (rows_all, LANES), 1)
    f_di = jnp.where(lane < 64, jnp.tanh(low[:, 0:LANES]), low[:, 0:LANES]).astype(BF16)
    lr = _dot(f_di, wdi_ref[...])
    z = w0_ref[...] + lr[:, 0:width]
    iclr = _sigmoid(a0_ref[...] + lr[:, width:2 * width])
    if g_ref is not None:
        g_ref[...] = (_dot(_sigmoid(low[:, LANES:]).astype(BF16), wgate_ref[...])
                      .astype(g_ref.dtype).reshape(g_ref.shape))

    logw = -math.exp(-0.5) * _sigmoid(z)

    kkr = k * kk_ref[...]
    kd = k * (1.0 + (iclr - 1.0) * ka_ref[...])
    ss, rkd = _head_sums([kkr * kkr, r * kd * rk_ref[...]], ones_ref[...])
    kk = kkr * lax.rsqrt(jnp.maximum(ss, 1e-24))
    a = -kk
    b = kk * iclr
    bonus_ref[...] = (rkd * v).astype(bonus_ref.dtype).reshape(bonus_ref.shape)

    hi = logw.astype(BF16)
    lo = (logw - hi.astype(F32)).astype(BF16)
    tri = tri_ref[...]
    cum = jnp.concatenate([_dot(tri, hi[bb * tb:(bb + 1) * tb]) + _dot(tri, lo[bb * tb:(bb + 1) * tb])
                           for bb in range(bps)], axis=0)

    e_incl = jnp.exp(cum)
    a_t = (a * jnp.exp(cum - logw)).astype(BF16)
    r_t = (r * e_incl).astype(BF16)
    e_neg = jnp.exp(-cum)
    b_t = (b * e_neg).astype(BF16)
    k_t = (kd * e_neg).astype(BF16)
    v_b = v.astype(BF16)

    shape_c = (CHUNK, GROUP_LANES)
    ti = lax.broadcasted_iota(jnp.int32, shape_c, 0)
    ji = lax.broadcasted_iota(jnp.int32, shape_c, 1) % CHUNK
    strict = (ji > ti) if reverse else (ji < ti)
    incl = (ji >= ti) if reverse else (ji <= ti)
    eye = jnp.where(ji == ti, 1.0, 0.0).astype(F32)
    same16 = (ti // 16) == (ji // 16)
    same32 = (ti // 32) == (ji // 32)
    lvl0 = strict & same16
    lvl1 = strict & same32 & jnp.logical_not(same16)
    lvl2 = strict & jnp.logical_not(same32)
    bi = lax.broadcasted_iota(jnp.int32, (GROUP_LANES, GROUP_LANES), 0) // HEAD_DIM
    bj = lax.broadcasted_iota(jnp.int32, (GROUP_LANES, GROUP_LANES), 1) // HEAD_DIM
    bd_mask = bi == bj

    def mm(x, y):
        return _dot(x.astype(BF16), _block_diag(y.astype(BF16), bd_mask))

    n_chunks = tb // CHUNK
    n_groups = width // GROUP_LANES
    order = range(n_chunks - 1, -1, -1) if reverse else range(n_chunks)
    chains = [(bb, c, gi) for bb in range(bps) for c in range(n_chunks) for gi in range(n_groups)]

    def part(arr, ch):
        bb, c, gi = ch
        r0 = bb * tb + c * CHUNK
        return arr[r0:r0 + CHUNK, gi * GROUP_LANES:(gi + 1) * GROUP_LANES]

    def each(fn, *dicts):
        return {ch: fn(*(d[ch] for d in dicts)) for ch in chains}

    bd = lambda y: _block_diag(y, bd_mask)
    bdf = lambda y: jnp.where(bd_mask, y, 0.0)

    at_c = {ch: part(a_t, ch) for ch in chains}
    rt_c = {ch: part(r_t, ch) for ch in chains}
    bk_c = {ch: jnp.concatenate([part(b_t, ch), part(k_t, ch)], axis=0) for ch in chains}
    v_c = {ch: part(v_b, ch) for ch in chains}
    ar = each(lambda p, q: jnp.concatenate([p, q], axis=0), at_c, rt_c)
    ab = each(lambda p, q: _dot_nt(p, bd(q[0:CHUNK])), ar, bk_c)
    ak = each(lambda p, q: _dot_nt(p, bd(q[CHUNK:])), ar, bk_c)
    a_rb = each(lambda p: jnp.where(incl, p[CHUNK:], 0.0).astype(BF16), ab)
    a_kk = each(lambda p: jnp.concatenate([jnp.where(strict, p[0:CHUNK], 0.0),
                                           jnp.where(incl, p[CHUNK:], 0.0)], axis=0).astype(BF16), ak)
    akv = each(lambda p, q: _dot(p, bd(q)), a_kk, v_c)
    n0 = each(lambda p: jnp.where(lvl0, p[0:CHUNK], 0.0), ab)
    pw = each(lambda p: mm(p, p), n0)
    x = each(lambda p: eye + p, n0)
    for _ in range(2):
        both = each(lambda p, q: mm(jnp.concatenate([p, q], axis=0), q), x, pw)
        x = each(lambda p, q: p + q[0:CHUNK], x, both)
        pw = each(lambda q: q[CHUNK:], both)
    x = each(lambda p, q: p + mm(p, q), x, pw)
    for lvl in (lvl1, lvl2):
        y = each(lambda p, q: mm(p, jnp.where(lvl, q[0:CHUNK], 0.0)), x, ab)
        x = each(lambda p, q: p + mm(q, p), x, y)
    tinv = each(lambda p: p.astype(BF16), x)
    wu = each(lambda p, a_, k_: _dot(p, jnp.concatenate([bd(a_), bd(k_[0:CHUNK].astype(BF16))], axis=1))
              .astype(BF16), tinv, at_c, akv)
    qo = each(lambda a_, w_: _dot(a_, jnp.concatenate([bd(w_[:, 0:GROUP_LANES]), bd(w_[:, GROUP_LANES:])],
                                                      axis=1)), a_rb, wu)
    q_c = each(lambda r_, p: (r_.astype(F32) + p[:, 0:GROUP_LANES]).astype(BF16), rt_c, qo)
    o_c = each(lambda p, k_: p[:, GROUP_LANES:] + k_[CHUNK:], qo, akv)
    zeros_c = jnp.zeros((CHUNK, GROUP_LANES), BF16)
    mc = each(lambda b_, w_, v_: _dot_tn(b_, jnp.concatenate(
        [w_, jnp.concatenate([zeros_c, v_], axis=1)], axis=0)), bk_c, wu, v_c)

    edges = [bb * tb + (c * CHUNK if reverse else (c + 1) * CHUNK - 1)
             for bb in range(bps) for c in range(n_chunks)]
    pad_rows = -len(edges) % 8
    edge_rows = jnp.concatenate([e_incl[e:e + 1, :] for e in edges]
                                + ([jnp.zeros((pad_rows, width), F32)] if pad_rows else []), axis=0)
    edge_cols = edge_rows.T
    p_c = {(bb, c, gi): edge_cols[gi * GROUP_LANES:(gi + 1) * GROUP_LANES,
                                  bb * n_chunks + c:bb * n_chunks + c + 1] for bb, c, gi in chains}
    m_c = each(lambda p, d: (bdf(p[:, 0:GROUP_LANES]) * d).astype(BF16), mc, p_c)
    c_c = each(lambda p, d: bdf(p[:, GROUP_LANES:]) * d, mc, p_c)

    lines = [(bb, gi) for bb in range(bps) for gi in range(n_groups)]
    st = {ln: state_ref[ln[0] * n_groups + ln[1]] for ln in lines}
    for c in order:
        for bb, gi in lines:
            ch = (bb, c, gi)
            both = _dot(jnp.concatenate([m_c[ch], q_c[ch]], axis=0), st[bb, gi].astype(BF16))
            o_ref[bb, c * CHUNK:(c + 1) * CHUNK, gi * GROUP_LANES:(gi + 1) * GROUP_LANES] = (
                both[GROUP_LANES:] + o_c[ch]).astype(o_ref.dtype)
            st[bb, gi] = st[bb, gi] * p_c[ch] + both[0:GROUP_LANES] + c_c[ch]
    for bb, gi in lines:
        state_ref[bb * n_groups + gi] = st[bb, gi]


def _rwkv_call(us3, wdi, wgate, w0, a0, k_k, k_a, r_k, ones_blocks, tri, *, reverse, width):
    bsz, s, cols = us3.shape
    tb = SCAN_BLOCK
    nt = s // tb

    def blk(t):
        return (nt - 1 - t) if reverse else t

    kern = functools.partial(_rwkv_kernel, reverse=reverse, width=width)
    const2 = lambda b, t: (0, 0)
    n_out = 2 if reverse else 3
    out_sds = jax.ShapeDtypeStruct((bsz, s, width), BF16)
    bps = SCAN_BATCH_ROWS if bsz % SCAN_BATCH_ROWS == 0 else 1
    out_spec = pl.BlockSpec((bps, tb, width), lambda b, t: (b, blk(t), 0))
    return pl.pallas_call(
        kern,
        grid=(bsz // bps, nt),
        in_specs=[pl.BlockSpec((bps, tb, cols), lambda b, t: (b, blk(t), 0)),
                  pl.BlockSpec(wdi.shape, const2),
                  pl.BlockSpec(wgate.shape, const2),
                  pl.BlockSpec((1, width), const2),
                  pl.BlockSpec((1, width), const2),
                  pl.BlockSpec((1, width), const2),
                  pl.BlockSpec((1, width), const2),
                  pl.BlockSpec((1, width), const2),
                  pl.BlockSpec(ones_blocks.shape, const2),
                  pl.BlockSpec(tri.shape, const2)],
        out_specs=[out_spec] * n_out,
        out_shape=[out_sds] * n_out,
        scratch_shapes=[pltpu.VMEM((bps * (width // GROUP_LANES), GROUP_LANES, GROUP_LANES), F32)],
        compiler_params=_cparams(("parallel", "arbitrary")),
        name="rwkv_bwd" if reverse else "rwkv_fwd",
    )(us3, wdi, wgate, w0, a0, k_k, k_a, r_k, ones_blocks, tri)


def _na_kernel(q_ref, k_ref, v_ref, bias_ref, o_ref, *, rows):
    g = pl.program_id(1)
    band = NA_BAND_ROWS * GRID_W
    start_row = jnp.clip(g * NA_ROWS_PER_STEP - NA_KH // 2, 0, rows - NA_BAND_ROWS)
    start = pl.multiple_of(start_row * GRID_W, GRID_W)
    n_pairs = q_ref.shape[2] // LANES
    tq = q_ref.shape[1]
    first_head = lax.broadcasted_iota(jnp.int32, (tq, LANES), 1) < HEAD_DIM
    ones_cols = jnp.ones((band, LANES), BF16)
    for p in range(n_pairs):
        ls = slice(p * LANES, (p + 1) * LANES)
        qp = q_ref[0, :, ls]
        kp = k_ref[0, pl.ds(start, band), ls]
        vp = v_ref[0, pl.ds(start, band), ls]
        zero = jnp.zeros_like(qp)
        q2 = jnp.concatenate([jnp.where(first_head, qp, zero), jnp.where(first_head, zero, qp)], axis=0)
        sc = _dot_nt(q2, kp) + bias_ref[0, 2 * p:2 * p + 2].reshape(2 * tq, band)
        m = jnp.max(sc, axis=-1, keepdims=True)
        e = jnp.exp2((sc - m).astype(BF16))
        pv = _dot(e, jnp.concatenate([vp, ones_cols], axis=1))
        out = pv[:, 0:LANES] * (1.0 / pv[:, LANES:LANES + 1])
        o_ref[0, :, ls] = jnp.where(first_head, out[0:tq], out[tq:]).astype(o_ref.dtype)


def _na_call(qkv3, bias, width):
    bsz, s, _ = qkv3.shape
    rows = s // GRID_W
    tq = NA_ROWS_PER_STEP * GRID_W
    ng = rows // NA_ROWS_PER_STEP
    nh = width // HEAD_DIM

    def bias_idx(b, g):
        ty = jnp.where(g == 0, 0, jnp.where(g == ng - 1, 2, 1))
        return (ty, 0, 0, 0)

    return pl.pallas_call(
        functools.partial(_na_kernel, rows=rows),
        grid=(bsz, ng),
        in_specs=[pl.BlockSpec((1, tq, width), lambda b, g: (b, g, 0)),
                  pl.BlockSpec((1, s, width), lambda b, g: (b, 0, 1)),
                  pl.BlockSpec((1, s, width), lambda b, g: (b, 0, 2)),
                  pl.BlockSpec((1, nh, tq, NA_BAND_ROWS * GRID_W), bias_idx)],
        out_specs=pl.BlockSpec((1, tq, width), lambda b, g: (b, g, 0)),
        out_shape=jax.ShapeDtypeStruct((bsz, s, width), BF16),
        compiler_params=_cparams(("parallel", "arbitrary")),
        name="nbr_attn",
    )(qkv3, qkv3, qkv3, bias)


def _na_bias_tables(rpb, rows):
    ng = rows // NA_ROWS_PER_STEP
    kh = min(NA_KH, rows)
    nh, n_dr, n_dc = rpb.shape
    col = np.arange(GRID_W)
    cs = np.clip(col - NA_KW // 2, 0, GRID_W - NA_KW)
    ok_c = (col[None, :] >= cs[:, None]) & (col[None, :] < cs[:, None] + NA_KW)
    dc = col[None, :] - col[:, None] + (NA_KW - 1)
    onehot = (dc[None] == np.arange(n_dc)[:, None, None]) & ok_c[None]
    expand = jnp.asarray(onehot.reshape(n_dc, GRID_W * GRID_W), F32)
    by_dr = jnp.dot(rpb.reshape(nh * n_dr, n_dc), expand, precision=lax.Precision.HIGHEST)
    by_dr = jnp.where(ok_c, by_dr.reshape(nh, n_dr, GRID_W, GRID_W), NEG_BIG)
    outside = jnp.full((nh, GRID_W, GRID_W), NEG_BIG, F32)
    tables = []
    for g in (0, 1, ng - 1):
        bs = int(np.clip(g * NA_ROWS_PER_STEP - NA_KH // 2, 0, rows - NA_BAND_ROWS))
        q_rows = []
        for qi in range(g * NA_ROWS_PER_STEP, (g + 1) * NA_ROWS_PER_STEP):
            rs = int(np.clip(qi - kh // 2, 0, rows - kh))
            blocks = [by_dr[:, kr - qi + NA_KH - 1] if rs <= kr < rs + kh else outside
                      for kr in range(bs, bs + NA_BAND_ROWS)]
            q_rows.append(jnp.concatenate(blocks, axis=-1))
        tables.append(jnp.concatenate(q_rows, axis=1))
    return jnp.stack(tables) * LOG2E


def _merge_kernel(o0_ref, o1_ref, b0_ref, b1_ref, g_ref, yb_ref, gate_ref, x_ref, ones_ref,
                  gng_ref, gnb_ref, wa_ref, wb_ref, wo_ref, lg_ref, lb_ref, xo_ref, xob_ref, *,
                  alpha, d_model):
    ones_bd = ones_ref[...]
    inv_n = 1.0 / HEAD_DIM
    tm = x_ref.shape[0]
    parts = [slice(h * tm // ROW_SPLIT, (h + 1) * tm // ROW_SPLIT) for h in range(ROW_SPLIT)]
    o = [o0_ref[rs, :].astype(F32) + o1_ref[rs, :].astype(F32) for rs in parts]
    mean = [_dot(v.astype(BF16), ones_bd) * inv_n for v in o]
    oc = [v - mu for v, mu in zip(o, mean)]
    var = [_dot((v * v).astype(BF16), ones_bd) * inv_n for v in oc]
    on = [v * lax.rsqrt(s2 + GN_EPS) * gng_ref[...] + gnb_ref[...] for v, s2 in zip(oc, var)]
    ya = [(v + b0_ref[rs, :].astype(F32) + b1_ref[rs, :].astype(F32)) * g_ref[rs, :].astype(F32)
          for v, rs in zip(on, parts)]
    za = [_dot(v.astype(BF16), wa_ref[...]) for v in ya]
    zb = [_dot(yb_ref[rs, :], wb_ref[...]) for rs in parts]
    merged = [_sigmoid(gate_ref[rs, 0:d_model].astype(F32)) * a
              + _sigmoid(gate_ref[rs, d_model:].astype(F32)) * b for rs, a, b in zip(parts, za, zb)]
    y = [alpha * x_ref[rs, :] + _dot(v.astype(BF16), wo_ref[...]) for rs, v in zip(parts, merged)]
    for rs, yy in zip(parts, y):
        xn = _layer_norm_rows(yy, lg_ref[...], lb_ref[...])
        xo_ref[rs, :] = xn
        xob_ref[rs, :] = xn.astype(BF16)


def _merge_call(o0, o1, b0, b1, g, yb, gates, x, ones_bd, gn_g, gn_b, wa, wb, wo, ln_g, ln_b,
                alpha, tm):
    m, d = x.shape
    w = o0.shape[1]
    rowblk = lambda n: pl.BlockSpec((tm, n), lambda i: (i, 0))
    const = lambda shape: pl.BlockSpec(shape, lambda i: (0, 0))
    return pl.pallas_call(
        functools.partial(_merge_kernel, alpha=alpha, d_model=d),
        grid=(m // tm,),
        in_specs=[rowblk(w), rowblk(w), rowblk(w), rowblk(w), rowblk(w), rowblk(w),
                  rowblk(2 * d), rowblk(d), const(ones_bd.shape),
                  const((1, w)), const((1, w)), const(wa.shape), const(wb.shape), const(wo.shape),
                  const((1, d)), const((1, d))],
        out_specs=[rowblk(d), rowblk(d)],
        out_shape=[jax.ShapeDtypeStruct((m, d), F32), jax.ShapeDtypeStruct((m, d), BF16)],
        compiler_params=_cparams(("parallel",)),
        name="merge_out_ln",
    )(o0, o1, b0, b1, g, yb, gates, x, ones_bd, gn_g, gn_b, wa, wb, wo, ln_g, ln_b)


def _ffn_kernel(x_ref, xb_ref, wg_ref, wu_ref, wo_ref, lg_ref, lb_ref, xo_ref, xob_ref, *, alpha):
    xb = xb_ref[...]
    hg = _dot(xb, wg_ref[...])
    hu = _dot(xb, wu_ref[...])
    act = (hg * _sigmoid(hg) * hu).astype(BF16)
    y = alpha * x_ref[...] + _dot(act, wo_ref[...])
    xn = _layer_norm_rows(y, lg_ref[...], lb_ref[...])
    xo_ref[...] = xn
    xob_ref[...] = xn.astype(BF16)


def _ffn_call(x, xb, wg, wu, wo, ln_g, ln_b, alpha, tm):
    m, d = x.shape
    rowblk = pl.BlockSpec((tm, d), lambda i: (i, 0))
    const = lambda shape: pl.BlockSpec(shape, lambda i: (0, 0), pipeline_mode=pl.Buffered(1))
    return pl.pallas_call(
        functools.partial(_ffn_kernel, alpha=alpha),
        grid=(m // tm,),
        in_specs=[rowblk, rowblk, const(wg.shape), const(wu.shape), const(wo.shape),
                  pl.BlockSpec((1, d), lambda i: (0, 0)), pl.BlockSpec((1, d), lambda i: (0, 0))],
        out_specs=[rowblk, rowblk],
        out_shape=[jax.ShapeDtypeStruct((m, d), F32), jax.ShapeDtypeStruct((m, d), BF16)],
        compiler_params=_cparams(("parallel",)),
        name="swiglu_ln",
    )(x, xb, wg, wu, wo, ln_g, ln_b)


def _low_rank_weights(decay_up_d, iclr_up_d, gate_up, direction, width):
    r_dec, r_icl, r_gate = decay_up_d.shape[0], iclr_up_d.shape[0], gate_up.shape[0]
    w = jnp.zeros((LANES, 2 * width), F32)
    w = w.at[direction * r_dec:(direction + 1) * r_dec, 0:width].set(decay_up_d)
    w = w.at[64 + direction * r_icl:64 + (direction + 1) * r_icl, width:].set(iclr_up_d)
    wg = jnp.zeros((LANES, width), F32).at[0:r_gate].set(gate_up)
    return w.astype(BF16), wg.astype(BF16)


def kernel(x, ln_in_g, ln_in_b, w_in, shift_mu, decay_w0, decay_up, iclr_a0, iclr_up, gate_up,
           k_k, k_a, r_k, gn_g, gn_b, na_rpb, w_branch_rwkv, w_branch_na, w_out, ln1_g, ln1_b,
           w_ffn_in, w_ffn_out, ln2_g, ln2_b):
    bsz, s, d = x.shape
    depth = w_in.shape[0]
    width = k_k.shape[1]
    na_width = w_branch_na.shape[1]
    d_ff = w_ffn_out.shape[1]
    rwkv_cols = shift_mu.shape[2]
    rwkv_pad = 3 * width + 2 * LANES
    assert 2 * decay_up.shape[2] == 64 and 2 * iclr_up.shape[2] == 64 and gate_up.shape[1] <= LANES
    assert rwkv_cols == 3 * width + LANES + gate_up.shape[1] and s % SCAN_BLOCK == 0
    rows = s // GRID_W
    assert rows % NA_ROWS_PER_STEP == 0 and rows >= NA_BAND_ROWS and rows // NA_ROWS_PER_STEP >= 3
    alpha = (2.0 * depth) ** 0.25
    m = bsz * s
    tm = 512 if m % 512 == 0 else 256

    ones_bd = jnp.asarray(np.kron(np.eye(width // HEAD_DIM), np.ones((HEAD_DIM, HEAD_DIM))), BF16)
    ones_blocks = ones_bd[0:MXU_WIDTH, 0:MXU_WIDTH]
    pos = np.arange(SCAN_BLOCK)
    same_chunk = (pos[:, None] // CHUNK) == (pos[None, :] // CHUNK)
    tri_f = jnp.asarray(same_chunk & (pos[None, :] <= pos[:, None]), BF16)
    tri_r = jnp.asarray(same_chunk & (pos[None, :] >= pos[:, None]), BF16)

    x2, xb = _ln_call(x.reshape(m, d), ln_in_g, ln_in_b, tm)
    row = lambda p: p.reshape(1, -1)
    for l in range(depth):
        wl = w_in[l]
        w_r = jnp.pad(wl[:, :rwkv_cols], ((0, 0), (0, rwkv_pad - rwkv_cols))).astype(BF16)
        w_n = wl[:, rwkv_cols:rwkv_cols + 3 * na_width]
        w_n = jnp.concatenate([w_n[:, :na_width] * (HEAD_DIM ** -0.5 * LOG2E), w_n[:, na_width:]],
                              axis=1).astype(BF16)
        w_g = wl[:, rwkv_cols + 3 * na_width:].astype(BF16)
        mu = jnp.pad(shift_mu[l], ((0, 0), (0, rwkv_pad - rwkv_cols)))

        us, qkv, gates = _proj_call(xb, w_r, w_n, w_g, mu, tm, s)
        us3 = us.reshape(bsz, s, rwkv_pad)

        outs = []
        for direction, tri in ((0, tri_f), (1, tri_r)):
            wdi, wgate = _low_rank_weights(decay_up[l, direction], iclr_up[l, direction], gate_up[l],
                                           direction, width)
            outs.append(_rwkv_call(
                us3, wdi, wgate, row(decay_w0[l, direction]), row(iclr_a0[l, direction]),
                row(k_k[l]), row(k_a[l]), row(r_k[l]), ones_blocks, tri,
                reverse=bool(direction), width=width))
        (o0, b0, g0), (o1, b1) = outs

        bias = _na_bias_tables(na_rpb[l], rows)
        yb = _na_call(qkv.reshape(bsz, s, 3 * na_width), bias, na_width)

        flat = lambda t: t.reshape(m, -1)
        x2, xb = _merge_call(
            flat(o0), flat(o1), flat(b0), flat(b1), flat(g0), flat(yb), gates, x2, ones_bd,
            row(gn_g[l]), row(gn_b[l]), w_branch_rwkv[l].astype(BF16), w_branch_na[l].astype(BF16),
            w_out[l].astype(BF16), row(ln1_g[l]), row(ln1_b[l]), alpha, tm)

        x2, xb = _ffn_call(
            x2, xb, w_ffn_in[l, :, :d_ff].astype(BF16), w_ffn_in[l, :, d_ff:].astype(BF16),
            w_ffn_out[l].astype(BF16), row(ln2_g[l]), row(ln2_b[l]), alpha, tm)
    return x2.reshape(bsz, s, d)
```

```python
import functools
import math

import jax
import jax.numpy as jnp
import numpy as np
from jax import lax
from jax.experimental import pallas as pl
from jax.experimental.pallas import tpu as pltpu

F32 = jnp.float32
BF16 = jnp.bfloat16

HEAD_DIM = 64
GRID_W = 64
NA_KH = 8
NA_KW = 16
NA_ROWS_PER_STEP = 4
NA_BAND_ROWS = NA_KH + NA_ROWS_PER_STEP - 1
LN_EPS = 1e-5
GN_EPS = 64e-5
NEG_BIG = -1e30
LOG2E = math.log2(math.e)
LANES = 128
MXU_WIDTH = 256
CHUNK = 64
SCAN_BLOCK = 256
SCAN_BATCH_ROWS = 2
HEADS_PER_GROUP = 2
GROUP_LANES = HEADS_PER_GROUP * HEAD_DIM
ROW_SPLIT = 2
VMEM_LIMIT = 56 * 1024 * 1024


def _cparams(sem):
    return pltpu.CompilerParams(dimension_semantics=sem, vmem_limit_bytes=VMEM_LIMIT)


def _dot(a, b):
    return jnp.dot(a, b, preferred_element_type=F32)


def _dot_nt(a, b):
    return lax.dot_general(a, b, (((1,), (1,)), ((), ())), preferred_element_type=F32)


def _dot_tn(a, b):
    return lax.dot_general(a, b, (((0,), (0,)), ((), ())), preferred_element_type=F32)


def _layer_norm_rows(x, g, b):
    mean = jnp.mean(x, axis=-1, keepdims=True)
    xc = x - mean
    var = jnp.mean(xc * xc, axis=-1, keepdims=True)
    return xc * lax.rsqrt(var + LN_EPS) * g + b


def _sigmoid(x):
    return 1.0 / (1.0 + jnp.exp(-x))


def _ln_kernel(x_ref, g_ref, b_ref, o_ref, ob_ref):
    y = _layer_norm_rows(x_ref[...], g_ref[...], b_ref[...])
    o_ref[...] = y
    ob_ref[...] = y.astype(BF16)


def _ln_call(x, g, b, tm):
    m, d = x.shape
    return pl.pallas_call(
        _ln_kernel,
        grid=(m // tm,),
        in_specs=[pl.BlockSpec((tm, d), lambda i: (i, 0)),
                  pl.BlockSpec((1, d), lambda i: (0, 0)),
                  pl.BlockSpec((1, d), lambda i: (0, 0))],
        out_specs=[pl.BlockSpec((tm, d), lambda i: (i, 0)),
                   pl.BlockSpec((tm, d), lambda i: (i, 0))],
        out_shape=[jax.ShapeDtypeStruct((m, d), F32), jax.ShapeDtypeStruct((m, d), BF16)],
        compiler_params=_cparams(("parallel",)),
        name="input_ln",
    )(x, g.reshape(1, d), b.reshape(1, d))


def _proj_kernel(x_ref, xp_ref, xn_ref, wr_ref, wn_ref, wg_ref, mu_ref, us_ref, qkv_ref, gate_ref, *,
                 blocks_per_seq):
    i = pl.program_id(0)
    x = x_ref[...]
    tm = x.shape[0]
    h = xp_ref.shape[0]
    xp = jnp.where(i % blocks_per_seq == 0, jnp.zeros_like(xp_ref[...]), xp_ref[...])
    xn = jnp.where(i % blocks_per_seq == blocks_per_seq - 1, jnp.zeros_like(xn_ref[...]), xn_ref[...])
    ext = _dot(jnp.concatenate([xp, x, xn], axis=0), wr_ref[...])
    keep = 1.0 - mu_ref[0:1, :] - mu_ref[1:2, :]
    us_ref[...] = (keep * ext[h:h + tm] + mu_ref[0:1, :] * ext[h - 1:h - 1 + tm]
                   + mu_ref[1:2, :] * ext[h + 1:h + 1 + tm])
    qkv_ref[...] = _dot(x, wn_ref[...]).astype(BF16)
    gate_ref[...] = _dot(x, wg_ref[...]).astype(BF16)


def _proj_call(xb, w_rwkv, w_na, w_gate, mu, tm, seq):
    m, d = xb.shape
    nr, nn, ng = w_rwkv.shape[1], w_na.shape[1], w_gate.shape[1]
    const = lambda i: (0, 0)
    halo = 16
    hb = tm // halo
    last = m // halo - 1
    return pl.pallas_call(
        functools.partial(_proj_kernel, blocks_per_seq=seq // tm),
        grid=(m // tm,),
        in_specs=[pl.BlockSpec((tm, d), lambda i: (i, 0)),
                  pl.BlockSpec((halo, d), lambda i: (jnp.maximum(i * hb - 1, 0), 0)),
                  pl.BlockSpec((halo, d), lambda i: (jnp.minimum((i + 1) * hb, last), 0)),
                  pl.BlockSpec((d, nr), const),
                  pl.BlockSpec((d, nn), const),
                  pl.BlockSpec((d, ng), const),
                  pl.BlockSpec(mu.shape, const)],
        out_specs=[pl.BlockSpec((tm, nr), lambda i: (i, 0)),
                   pl.BlockSpec((tm, nn), lambda i: (i, 0)),
                   pl.BlockSpec((tm, ng), lambda i: (i, 0))],
        out_shape=[jax.ShapeDtypeStruct((m, nr), F32),
                   jax.ShapeDtypeStruct((m, nn), BF16),
                   jax.ShapeDtypeStruct((m, ng), BF16)],
        compiler_params=_cparams(("parallel",)),
        name="in_proj",
    )(xb, xb, xb, w_rwkv, w_na, w_gate, mu)


def _block_diag(x, bd_mask):
    tiled = jnp.concatenate([x] * HEADS_PER_GROUP, axis=0)
    return jnp.where(bd_mask, tiled, jnp.zeros_like(tiled))


def _head_sums(xs, ones_blocks):
    rows, width = xs[0].shape
    gw = ones_blocks.shape[0]
    n_g = width // gw
    stacked = jnp.concatenate([x[:, i * gw:(i + 1) * gw] for x in xs for i in range(n_g)], axis=0)
    res = _dot(stacked.astype(BF16), ones_blocks)
    return [jnp.concatenate([res[(j * n_g + i) * rows:(j * n_g + i + 1) * rows] for i in range(n_g)],
                            axis=1) for j in range(len(xs))]


def _rwkv_kernel(us_ref, wdi_ref, wgate_ref, w0_ref, a0_ref, kk_ref, ka_ref, rk_ref, ones_ref, tri_ref,
                 o_ref, bonus_ref, *rest, reverse, width):
    g_ref = None if reverse else rest[0]
    state_ref = rest[-1]
    t = pl.program_id(1)
    bps, tb = us_ref.shape[0], us_ref.shape[1]
    rows_all = bps * tb
    c3 = 3 * width

    @pl.when(t == 0)
    def _():
        state_ref[...] = jnp.zeros_like(state_ref)

    us = us_ref[...].reshape(rows_all, us_ref.shape[2])
    r = us[:, 0:width]
    k = us[:, width:2 * width]
    v = us[:, 2 * width:c3]
    low = us[:, c3:c3 + 2 * LANES]

    lane = lax.broadcasted_iota(jnp.int32, (rows_all, LANES), 1)
    f_di = jnp.where(lane < 64, jnp.tanh(low[:, 0:LANES]), low[:, 0:LANES]).astype(BF16)
    lr = _dot(f_di, wdi_ref[...])
    z = w0_ref[...] + lr[:, 0:width]
    iclr = _sigmoid(a0_ref[...] + lr[:, width:2 * width])
    if g_ref is not None:
        g_ref[...] = (_dot(_sigmoid(low[:, LANES:]).astype(BF16), wgate_ref[...])
                      .astype(g_ref.dtype).reshape(g_ref.shape))

    logw = -math.exp(-0.5) * _sigmoid(z)

    kkr = k * kk_ref[...]
    kd = k * (1.0 + (iclr - 1.0) * ka_ref[...])
    ss, rkd = _head_sums([kkr * kkr, r * kd * rk_ref[...]], ones_ref[...])
    kk = kkr * lax.rsqrt(jnp.maximum(ss, 1e-24))
    a = -kk
    b = kk * iclr
    bonus_ref[...] = (rkd * v).astype(bonus_ref.dtype).reshape(bonus_ref.shape)

    hi = logw.astype(BF16)
    lo = (logw - hi.astype(F32)).astype(BF16)
    tri = tri_ref[...]
    cum = jnp.concatenate([_dot(tri, hi[bb * tb:(bb + 1) * tb]) + _dot(tri, lo[bb * tb:(bb + 1) * tb])
                           for bb in range(bps)], axis=0)

    e_incl = jnp.exp(cum)
    a_t = (a * jnp.exp(cum - logw)).astype(BF16)
    r_t = (r * e_incl).astype(BF16)
    e_neg = jnp.exp(-cum)
    b_t = (b * e_neg).astype(BF16)
    k_t = (kd * e_neg).astype(BF16)
    v_b = v.astype(BF16)

    shape_c = (CHUNK, GROUP_LANES)
    ti = lax.broadcasted_iota(jnp.int32, shape_c, 0)
    ji = lax.broadcasted_iota(jnp.int32, shape_c, 1) % CHUNK
    strict = (ji > ti) if reverse else (ji < ti)
    incl = (ji >= ti) if reverse else (ji <= ti)
    eye = jnp.where(ji == ti, 1.0, 0.0).astype(F32)
    same16 = (ti // 16) == (ji // 16)
    same32 = (ti // 32) == (ji // 32)
    lvl0 = strict & same16
    lvl1 = strict & same32 & jnp.logical_not(same16)
    lvl2 = strict & jnp.logical_not(same32)
    bi = lax.broadcasted_iota(jnp.int32, (GROUP_LANES, GROUP_LANES), 0) // HEAD_DIM
    bj = lax.broadcasted_iota(jnp.int32, (GROUP_LANES, GROUP_LANES), 1) // HEAD_DIM
    bd_mask = bi == bj

    def mm(x, y):
        return _dot(x.astype(BF16), _block_diag(y.astype(BF16), bd_mask))

    n_chunks = tb // CHUNK
    n_groups = width // GROUP_LANES
    order = range(n_chunks - 1, -1, -1) if reverse else range(n_chunks)
    chains = [(bb, c, gi) for bb in range(bps) for c in range(n_chunks) for gi in range(n_groups)]

    def part(arr, ch):
        bb, c, gi = ch
        r0 = bb * tb + c * CHUNK
        return arr[r0:r0 + CHUNK, gi * GROUP_LANES:(gi + 1) * GROUP_LANES]

    def each(fn, *dicts):
        return {ch: fn(*(d[ch] for d in dicts)) for ch in chains}

    bd = lambda y: _block_diag(y, bd_mask)
    bdf = lambda y: jnp.where(bd_mask, y, 0.0)

    at_c = {ch: part(a_t, ch) for ch in chains}
    rt_c = {ch: part(r_t, ch) for ch in chains}
    bk_c = {ch: jnp.concatenate([part(b_t, ch), part(k_t, ch)], axis=0) for ch in chains}
    v_c = {ch: part(v_b, ch) for ch in chains}
    ar = each(lambda p, q: jnp.concatenate([p, q], axis=0), at_c, rt_c)
    ab = each(lambda p, q: _dot_nt(p, bd(q[0:CHUNK])), ar, bk_c)
    ak = each(lambda p, q: _dot_nt(p, bd(q[CHUNK:])), ar, bk_c)
    a_rb = each(lambda p: jnp.where(incl, p[CHUNK:], 0.0).astype(BF16), ab)
    a_kk = each(lambda p: jnp.concatenate([jnp.where(strict, p[0:CHUNK], 0.0),
                                           jnp.where(incl, p[CHUNK:], 0.0)], axis=0).astype(BF16), ak)
    akv = each(lambda p, q: _dot(p, bd(q)), a_kk, v_c)
    n0 = each(lambda p: jnp.where(lvl0, p[0:CHUNK], 0.0), ab)
    pw = each(lambda p: mm(p, p), n0)
    x = each(lambda p: eye + p, n0)
    for _ in range(2):
        both = each(lambda p, q: mm(jnp.concatenate([p, q], axis=0), q), x, pw)
        x = each(lambda p, q: p + q[0:CHUNK], x, both)
        pw = each(lambda q: q[CHUNK:], both)
    x = each(lambda p, q: p + mm(p, q), x, pw)
    for lvl in (lvl1, lvl2):
        y = each(lambda p, q: mm(p, jnp.where(lvl, q[0:CHUNK], 0.0)), x, ab)
        x = each(lambda p, q: p + mm(q, p), x, y)
    tinv = each(lambda p: p.astype(BF16), x)
    wu = each(lambda p, a_, k_: _dot(p, jnp.concatenate([bd(a_), bd(k_[0:CHUNK].astype(BF16))], axis=1))
              .astype(BF16), tinv, at_c, akv)
    qo = each(lambda a_, w_: _dot(a_, jnp.concatenate([bd(w_[:, 0:GROUP_LANES]), bd(w_[:, GROUP_LANES:])],
                                                      axis=1)), a_rb, wu)
    q_c = each(lambda r_, p: (r_.astype(F32) + p[:, 0:GROUP_LANES]).astype(BF16), rt_c, qo)
    o_c = each(lambda p, k_: p[:, GROUP_LANES:] + k_[CHUNK:], qo, akv)
    zeros_c = jnp.zeros((CHUNK, GROUP_LANES), BF16)
    mc = each(lambda b_, w_, v_: _dot_tn(b_, jnp.concatenate(
        [w_, jnp.concatenate([zeros_c, v_], axis=1)], axis=0)), bk_c, wu, v_c)

    edges = [bb * tb + (c * CHUNK if reverse else (c + 1) * CHUNK - 1)
             for bb in range(bps) for c in range(n_chunks)]
    pad_rows = -len(edges) % 8
    edge_rows = jnp.concatenate([e_incl[e:e + 1, :] for e in edges]
                                + ([jnp.zeros((pad_rows, width), F32)] if pad_rows else []), axis=0)
    edge_cols = edge_rows.T
    p_c = {(bb, c, gi): edge_cols[gi * GROUP_LANES:(gi + 1) * GROUP_LANES,
                                  bb * n_chunks + c:bb * n_chunks + c + 1] for bb, c, gi in chains}
    m_c = each(lambda p, d: (bdf(p[:, 0:GROUP_LANES]) * d).astype(BF16), mc, p_c)
    c_c = each(lambda p, d: bdf(p[:, GROUP_LANES:]) * d, mc, p_c)

    lines = [(bb, gi) for bb in range(bps) for gi in range(n_groups)]
    st = {ln: state_ref[ln[0] * n_groups + ln[1]] for ln in lines}
    for c in order:
        for bb, gi in lines:
            ch = (bb, c, gi)
            both = _dot(jnp.concatenate([m_c[ch], q_c[ch]], axis=0), st[bb, gi].astype(BF16))
            o_ref[bb, c * CHUNK:(c + 1) * CHUNK, gi * GROUP_LANES:(gi + 1) * GROUP_LANES] = (
                both[GROUP_LANES:] + o_c[ch]).astype(o_ref.dtype)
            st[bb, gi] = st[bb, gi] * p_c[ch] + both[0:GROUP_LANES] + c_c[ch]
    for bb, gi in lines:
        state_ref[bb * n_groups + gi] = st[bb, gi]


def _rwkv_call(us3, wdi, wgate, w0, a0, k_k, k_a, r_k, ones_blocks, tri, *, reverse, width):
    bsz, s, cols = us3.shape
    tb = SCAN_BLOCK
    nt = s // tb

    def blk(t):
        return (nt - 1 - t) if reverse else t

    kern = functools.partial(_rwkv_kernel, reverse=reverse, width=width)
    const2 = lambda b, t: (0, 0)
    n_out = 2 if reverse else 3
    out_sds = jax.ShapeDtypeStruct((bsz, s, width), BF16)
    bps = SCAN_BATCH_ROWS if bsz % SCAN_BATCH_ROWS == 0 else 1
    out_spec = pl.BlockSpec((bps, tb, width), lambda b, t: (b, blk(t), 0))
    return pl.pallas_call(
        kern,
        grid=(bsz // bps, nt),
        in_specs=[pl.BlockSpec((bps, tb, cols), lambda b, t: (b, blk(t), 0)),
                  pl.BlockSpec(wdi.shape, const2),
                  pl.BlockSpec(wgate.shape, const2),
                  pl.BlockSpec((1, width), const2),
                  pl.BlockSpec((1, width), const2),
                  pl.BlockSpec((1, width), const2),
                  pl.BlockSpec((1, width), const2),
                  pl.BlockSpec((1, width), const2),
                  pl.BlockSpec(ones_blocks.shape, const2),
                  pl.BlockSpec(tri.shape, const2)],
        out_specs=[out_spec] * n_out,
        out_shape=[out_sds] * n_out,
        scratch_shapes=[pltpu.VMEM((bps * (width // GROUP_LANES), GROUP_LANES, GROUP_LANES), F32)],
        compiler_params=_cparams(("parallel", "arbitrary")),
        name="rwkv_bwd" if reverse else "rwkv_fwd",
    )(us3, wdi, wgate, w0, a0, k_k, k_a, r_k, ones_blocks, tri)


def _na_kernel(q_ref, k_ref, v_ref, bias_ref, o_ref, *, rows):
    g = pl.program_id(1)
    band = NA_BAND_ROWS * GRID_W
    start_row = jnp.clip(g * NA_ROWS_PER_STEP - NA_KH // 2, 0, rows - NA_BAND_ROWS)
    start = pl.multiple_of(start_row * GRID_W, GRID_W)
    n_pairs = q_ref.shape[2] // LANES
    tq = q_ref.shape[1]
    first_head = lax.broadcasted_iota(jnp.int32, (tq, LANES), 1) < HEAD_DIM
    ones_cols = jnp.ones((band, LANES), BF16)
    for p in range(n_pairs):
        ls = slice(p * LANES, (p + 1) * LANES)
        qp = q_ref[0, :, ls]
        kp = k_ref[0, pl.ds(start, band), ls]
        vp = v_ref[0, pl.ds(start, band), ls]
        zero = jnp.zeros_like(qp)
        q2 = jnp.concatenate([jnp.where(first_head, qp, zero), jnp.where(first_head, zero, qp)], axis=0)
        sc = _dot_nt(q2, kp) + bias_ref[0, 2 * p:2 * p + 2].reshape(2 * tq, band)
        m = jnp.max(sc, axis=-1, keepdims=True)
        e = jnp.exp2((sc - m).astype(BF16))
        pv = _dot(e, jnp.concatenate([vp, ones_cols], axis=1))
        out = pv[:, 0:LANES] * (1.0 / pv[:, LANES:LANES + 1])
        o_ref[0, :, ls] = jnp.where(first_head, out[0:tq], out[tq:]).astype(o_ref.dtype)


def _na_call(qkv3, bias, layer, width):
    bsz, s, _ = qkv3.shape
    rows = s // GRID_W
    tq = NA_ROWS_PER_STEP * GRID_W
    ng = rows // NA_ROWS_PER_STEP
    nh = width // HEAD_DIM

    def bias_idx(b, g):
        ty = jnp.where(g == 0, 0, jnp.where(g == ng - 1, 2, 1))
        return (ty, layer, 0, 0)

    return pl.pallas_call(
        functools.partial(_na_kernel, rows=rows),
        grid=(bsz, ng),
        in_specs=[pl.BlockSpec((1, tq, width), lambda b, g: (b, g, 0)),
                  pl.BlockSpec((1, s, width), lambda b, g: (b, 0, 1)),
                  pl.BlockSpec((1, s, width), lambda b, g: (b, 0, 2)),
                  pl.BlockSpec((1, nh, tq, NA_BAND_ROWS * GRID_W), bias_idx)],
        out_specs=pl.BlockSpec((1, tq, width), lambda b, g: (b, g, 0)),
        out_shape=jax.ShapeDtypeStruct((bsz, s, width), BF16),
        compiler_params=_cparams(("parallel", "arbitrary")),
        name="nbr_attn",
    )(qkv3, qkv3, qkv3, bias)


def _na_bias_tables(rpb, rows):
    ng = rows // NA_ROWS_PER_STEP
    kh = min(NA_KH, rows)
    nh, n_dr, n_dc = rpb.shape
    col = np.arange(GRID_W)
    cs = np.clip(col - NA_KW // 2, 0, GRID_W - NA_KW)
    ok_c = (col[None, :] >= cs[:, None]) & (col[None, :] < cs[:, None] + NA_KW)
    dc = col[None, :] - col[:, None] + (NA_KW - 1)
    onehot = (dc[None] == np.arange(n_dc)[:, None, None]) & ok_c[None]
    expand = jnp.asarray(onehot.reshape(n_dc, GRID_W * GRID_W), F32)
    by_dr = jnp.dot(rpb.reshape(nh * n_dr, n_dc), expand, precision=lax.Precision.HIGHEST)
    by_dr = jnp.where(ok_c, by_dr.reshape(nh, n_dr, GRID_W, GRID_W), NEG_BIG)
    outside = jnp.full((nh, GRID_W, GRID_W), NEG_BIG, F32)
    tables = []
    for g in (0, 1, ng - 1):
        bs = int(np.clip(g * NA_ROWS_PER_STEP - NA_KH // 2, 0, rows - NA_BAND_ROWS))
        q_rows = []
        for qi in range(g * NA_ROWS_PER_STEP, (g + 1) * NA_ROWS_PER_STEP):
            rs = int(np.clip(qi - kh // 2, 0, rows - kh))
            blocks = [by_dr[:, kr - qi + NA_KH - 1] if rs <= kr < rs + kh else outside
                      for kr in range(bs, bs + NA_BAND_ROWS)]
            q_rows.append(jnp.concatenate(blocks, axis=-1))
        tables.append(jnp.concatenate(q_rows, axis=1))
    return jnp.stack(tables) * LOG2E


def _merge_kernel(o0_ref, o1_ref, b0_ref, b1_ref, g_ref, yb_ref, gate_ref, x_ref, ones_ref,
                  gng_ref, gnb_ref, wa_ref, wb_ref, wo_ref, lg_ref, lb_ref, xo_ref, xob_ref, *,
                  alpha, d_model):
    ones_bd = ones_ref[...]
    inv_n = 1.0 / HEAD_DIM
    tm = x_ref.shape[0]
    parts = [slice(h * tm // ROW_SPLIT, (h + 1) * tm // ROW_SPLIT) for h in range(ROW_SPLIT)]
    o = [o0_ref[rs, :].astype(F32) + o1_ref[rs, :].astype(F32) for rs in parts]
    mean = [_dot(v.astype(BF16), ones_bd) * inv_n for v in o]
    oc = [v - mu for v, mu in zip(o, mean)]
    var = [_dot((v * v).astype(BF16), ones_bd) * inv_n for v in oc]
    on = [v * lax.rsqrt(s2 + GN_EPS) * gng_ref[...] + gnb_ref[...] for v, s2 in zip(oc, var)]
    ya = [(v + b0_ref[rs, :].astype(F32) + b1_ref[rs, :].astype(F32)) * g_ref[rs, :].astype(F32)
          for v, rs in zip(on, parts)]
    za = [_dot(v.astype(BF16), wa_ref[...]) for v in ya]
    zb = [_dot(yb_ref[rs, :], wb_ref[...]) for rs in parts]
    merged = [_sigmoid(gate_ref[rs, 0:d_model].astype(F32)) * a
              + _sigmoid(gate_ref[rs, d_model:].astype(F32)) * b for rs, a, b in zip(parts, za, zb)]
    y = [alpha * x_ref[rs, :] + _dot(v.astype(BF16), wo_ref[...]) for rs, v in zip(parts, merged)]
    for rs, yy in zip(parts, y):
        xn = _layer_norm_rows(yy, lg_ref[...], lb_ref[...])
        xo_ref[rs, :] = xn
        xob_ref[rs, :] = xn.astype(BF16)


def _merge_call(o0, o1, b0, b1, g, yb, gates, x, ones_bd, gn_g, gn_b, wa, wb, wo, ln_g, ln_b,
                alpha, tm):
    m, d = x.shape
    w = o0.shape[1]
    rowblk = lambda n: pl.BlockSpec((tm, n), lambda i: (i, 0))
    const = lambda shape: pl.BlockSpec(shape, lambda i: (0, 0))
    return pl.pallas_call(
        functools.partial(_merge_kernel, alpha=alpha, d_model=d),
        grid=(m // tm,),
        in_specs=[rowblk(w), rowblk(w), rowblk(w), rowblk(w), rowblk(w), rowblk(w),
                  rowblk(2 * d), rowblk(d), const(ones_bd.shape),
                  const((1, w)), const((1, w)), const(wa.shape), const(wb.shape), const(wo.shape),
                  const((1, d)), const((1, d))],
        out_specs=[rowblk(d), rowblk(d)],
        out_shape=[jax.ShapeDtypeStruct((m, d), F32), jax.ShapeDtypeStruct((m, d), BF16)],
        compiler_params=_cparams(("parallel",)),
        name="merge_out_ln",
    )(o0, o1, b0, b1, g, yb, gates, x, ones_bd, gn_g, gn_b, wa, wb, wo, ln_g, ln_b)


def _ffn_kernel(x_ref, xb_ref, wg_ref, wu_ref, wo_ref, lg_ref, lb_ref, xo_ref, xob_ref, *, alpha):
    xb = xb_ref[...]
    hg = _dot(xb, wg_ref[...])
    hu = _dot(xb, wu_ref[...])
    act = (hg * _sigmoid(hg) * hu).astype(BF16)
    y = alpha * x_ref[...] + _dot(act, wo_ref[...])
    xn = _layer_norm_rows(y, lg_ref[...], lb_ref[...])
    xo_ref[...] = xn
    xob_ref[...] = xn.astype(BF16)


def _ffn_call(x, xb, wg, wu, wo, ln_g, ln_b, alpha, tm):
    m, d = x.shape
    rowblk = pl.BlockSpec((tm, d), lambda i: (i, 0))
    const = lambda shape: pl.BlockSpec(shape, lambda i: (0, 0), pipeline_mode=pl.Buffered(1))
    return pl.pallas_call(
        functools.partial(_ffn_kernel, alpha=alpha),
        grid=(m // tm,),
        in_specs=[rowblk, rowblk, const(wg.shape), const(wu.shape), const(wo.shape),
                  pl.BlockSpec((1, d), lambda i: (0, 0)), pl.BlockSpec((1, d), lambda i: (0, 0))],
        out_specs=[rowblk, rowblk],
        out_shape=[jax.ShapeDtypeStruct((m, d), F32), jax.ShapeDtypeStruct((m, d), BF16)],
        compiler_params=_cparams(("parallel",)),
        name="swiglu_ln",
    )(x, xb, wg, wu, wo, ln_g, ln_b)


def _low_rank_weights(decay_up_d, iclr_up_d, gate_up, direction, width):
    r_dec, r_icl, r_gate = decay_up_d.shape[0], iclr_up_d.shape[0], gate_up.shape[0]
    w = jnp.zeros((LANES, 2 * width), F32)
    w = w.at[direction * r_dec:(direction + 1) * r_dec, 0:width].set(decay_up_d)
    w = w.at[64 + direction * r_icl:64 + (direction + 1) * r_icl, width:].set(iclr_up_d)
    wg = jnp.zeros((LANES, width), F32).at[0:r_gate].set(gate_up)
    return w.astype(BF16), wg.astype(BF16)


def kernel(x, ln_in_g, ln_in_b, w_in, shift_mu, decay_w0, decay_up, iclr_a0, iclr_up, gate_up,
           k_k, k_a, r_k, gn_g, gn_b, na_rpb, w_branch_rwkv, w_branch_na, w_out, ln1_g, ln1_b,
           w_ffn_in, w_ffn_out, ln2_g, ln2_b):
    bsz, s, d = x.shape
    depth = w_in.shape[0]
    width = k_k.shape[1]
    na_width = w_branch_na.shape[1]
    d_ff = w_ffn_out.shape[1]
    rwkv_cols = shift_mu.shape[2]
    rwkv_pad = 3 * width + 2 * LANES
    assert 2 * decay_up.shape[2] == 64 and 2 * iclr_up.shape[2] == 64 and gate_up.shape[1] <= LANES
    assert rwkv_cols == 3 * width + LANES + gate_up.shape[1] and s % SCAN_BLOCK == 0
    rows = s // GRID_W
    assert rows % NA_ROWS_PER_STEP == 0 and rows >= NA_BAND_ROWS and rows // NA_ROWS_PER_STEP >= 3
    alpha = (2.0 * depth) ** 0.25
    m = bsz * s
    tm = 512 if m % 512 == 0 else 256

    ones_bd = jnp.asarray(np.kron(np.eye(width // HEAD_DIM), np.ones((HEAD_DIM, HEAD_DIM))), BF16)
    ones_blocks = ones_bd[0:MXU_WIDTH, 0:MXU_WIDTH]
    pos = np.arange(SCAN_BLOCK)
    same_chunk = (pos[:, None] // CHUNK) == (pos[None, :] // CHUNK)
    tri_f = jnp.asarray(same_chunk & (pos[None, :] <= pos[:, None]), BF16)
    tri_r = jnp.asarray(same_chunk & (pos[None, :] >= pos[:, None]), BF16)

    na_bias = _na_bias_tables(na_rpb.reshape((-1,) + na_rpb.shape[2:]), rows)

    x2, xb = _ln_call(x.reshape(m, d), ln_in_g, ln_in_b, tm)
    row = lambda p: p.reshape(1, -1)
    for l in range(depth):
        wl = w_in[l]
        w_r = jnp.pad(wl[:, :rwkv_cols], ((0, 0), (0, rwkv_pad - rwkv_cols))).astype(BF16)
        w_n = wl[:, rwkv_cols:rwkv_cols + 3 * na_width]
        w_n = jnp.concatenate([w_n[:, :na_width] * (HEAD_DIM ** -0.5 * LOG2E), w_n[:, na_width:]],
                              axis=1).astype(BF16)
        w_g = wl[:, rwkv_cols + 3 * na_width:].astype(BF16)
        mu = jnp.pad(shift_mu[l], ((0, 0), (0, rwkv_pad - rwkv_cols)))

        us, qkv, gates = _proj_call(xb, w_r, w_n, w_g, mu, tm, s)
        us3 = us.reshape(bsz, s, rwkv_pad)

        outs = []
        for direction, tri in ((0, tri_f), (1, tri_r)):
            wdi, wgate = _low_rank_weights(decay_up[l, direction], iclr_up[l, direction], gate_up[l],
                                           direction, width)
            outs.append(_rwkv_call(
                us3, wdi, wgate, row(decay_w0[l, direction]), row(iclr_a0[l, direction]),
                row(k_k[l]), row(k_a[l]), row(r_k[l]), ones_blocks, tri,
                reverse=bool(direction), width=width))
        (o0, b0, g0), (o1, b1) = outs

        yb = _na_call(qkv.reshape(bsz, s, 3 * na_width), na_bias, l, na_width)

        flat = lambda t: t.reshape(m, -1)
        x2, xb = _merge_call(
            flat(o0), flat(o1), flat(b0), flat(b1), flat(g0), flat(yb), gates, x2, ones_bd,
            row(gn_g[l]), row(gn_b[l]), w_branch_rwkv[l].astype(BF16), w_branch_na[l].astype(BF16),
            w_out[l].astype(BF16), row(ln1_g[l]), row(ln1_b[l]), alpha, tm)

        x2, xb = _ffn_call(
            x2, xb, w_ffn_in[l, :, :d_ff].astype(BF16), w_ffn_in[l, :, d_ff:].astype(BF16),
            w_ffn_out[l].astype(BF16), row(ln2_g[l]), row(ln2_b[l]), alpha, tm)
    return x2.reshape(bsz, s, d)
```

```python
import functools
import math

import jax
import jax.numpy as jnp
import numpy as np
from jax import lax
from jax.experimental import pallas as pl
from jax.experimental.pallas import tpu as pltpu

F32 = jnp.float32
BF16 = jnp.bfloat16

HEAD_DIM = 64
GRID_W = 64
NA_KH = 8
NA_KW = 16
NA_ROWS_PER_STEP = 4
NA_BAND_ROWS = NA_KH + NA_ROWS_PER_STEP - 1
NA_BATCH_ROWS = 2
LN_EPS = 1e-5
GN_EPS = 64e-5
NEG_BIG = -1e30
LOG2E = math.log2(math.e)
LANES = 128
MXU_WIDTH = 256
CHUNK = 64
SCAN_BLOCK = 256
SCAN_BATCH_ROWS = 2
HEADS_PER_GROUP = 2
GROUP_LANES = HEADS_PER_GROUP * HEAD_DIM
ROW_SPLIT = 2
VMEM_LIMIT = 56 * 1024 * 1024


def _cparams(sem):
    return pltpu.CompilerParams(dimension_semantics=sem, vmem_limit_bytes=VMEM_LIMIT)


def _dot(a, b):
    return jnp.dot(a, b, preferred_element_type=F32)


def _dot_nt(a, b):
    return lax.dot_general(a, b, (((1,), (1,)), ((), ())), preferred_element_type=F32)


def _dot_tn(a, b):
    return lax.dot_general(a, b, (((0,), (0,)), ((), ())), preferred_element_type=F32)


def _layer_norm_rows(x, g, b):
    mean = jnp.mean(x, axis=-1, keepdims=True)
    xc = x - mean
    var = jnp.mean(xc * xc, axis=-1, keepdims=True)
    return xc * lax.rsqrt(var + LN_EPS) * g + b


def _sigmoid(x):
    return 1.0 / (1.0 + jnp.exp(-x))


def _ln_kernel(x_ref, g_ref, b_ref, o_ref, ob_ref):
    y = _layer_norm_rows(x_ref[...], g_ref[...], b_ref[...])
    o_ref[...] = y
    ob_ref[...] = y.astype(BF16)


def _ln_call(x, g, b, tm):
    m, d = x.shape
    return pl.pallas_call(
        _ln_kernel,
        grid=(m // tm,),
        in_specs=[pl.BlockSpec((tm, d), lambda i: (i, 0)),
                  pl.BlockSpec((1, d), lambda i: (0, 0)),
                  pl.BlockSpec((1, d), lambda i: (0, 0))],
        out_specs=[pl.BlockSpec((tm, d), lambda i: (i, 0)),
                   pl.BlockSpec((tm, d), lambda i: (i, 0))],
        out_shape=[jax.ShapeDtypeStruct((m, d), F32), jax.ShapeDtypeStruct((m, d), BF16)],
        compiler_params=_cparams(("parallel",)),
        name="input_ln",
    )(x, g.reshape(1, d), b.reshape(1, d))


def _proj_kernel(x_ref, xp_ref, xn_ref, wr_ref, wn_ref, wg_ref, mu_ref, us_ref, qkv_ref, gate_ref, *,
                 blocks_per_seq):
    i = pl.program_id(0)
    x = x_ref[...]
    tm = x.shape[0]
    h = xp_ref.shape[0]
    xp = jnp.where(i % blocks_per_seq == 0, jnp.zeros_like(xp_ref[...]), xp_ref[...])
    xn = jnp.where(i % blocks_per_seq == blocks_per_seq - 1, jnp.zeros_like(xn_ref[...]), xn_ref[...])
    ext = _dot(jnp.concatenate([xp, x, xn], axis=0), wr_ref[...])
    keep = 1.0 - mu_ref[0:1, :] - mu_ref[1:2, :]
    us_ref[...] = (keep * ext[h:h + tm] + mu_ref[0:1, :] * ext[h - 1:h - 1 + tm]
                   + mu_ref[1:2, :] * ext[h + 1:h + 1 + tm])
    qkv_ref[...] = _dot(x, wn_ref[...]).astype(BF16)
    gate_ref[...] = _dot(x, wg_ref[...]).astype(BF16)


def _proj_call(xb, w_rwkv, w_na, w_gate, mu, tm, seq):
    m, d = xb.shape
    nr, nn, ng = w_rwkv.shape[1], w_na.shape[1], w_gate.shape[1]
    const = lambda i: (0, 0)
    halo = 16
    hb = tm // halo
    last = m // halo - 1
    return pl.pallas_call(
        functools.partial(_proj_kernel, blocks_per_seq=seq // tm),
        grid=(m // tm,),
        in_specs=[pl.BlockSpec((tm, d), lambda i: (i, 0)),
                  pl.BlockSpec((halo, d), lambda i: (jnp.maximum(i * hb - 1, 0), 0)),
                  pl.BlockSpec((halo, d), lambda i: (jnp.minimum((i + 1) * hb, last), 0)),
                  pl.BlockSpec((d, nr), const),
                  pl.BlockSpec((d, nn), const),
                  pl.BlockSpec((d, ng), const),
                  pl.BlockSpec(mu.shape, const)],
        out_specs=[pl.BlockSpec((tm, nr), lambda i: (i, 0)),
                   pl.BlockSpec((tm, nn), lambda i: (i, 0)),
                   pl.BlockSpec((tm, ng), lambda i: (i, 0))],
        out_shape=[jax.ShapeDtypeStruct((m, nr), F32),
                   jax.ShapeDtypeStruct((m, nn), BF16),
                   jax.ShapeDtypeStruct((m, ng), BF16)],
        compiler_params=_cparams(("parallel",)),
        name="in_proj",
    )(xb, xb, xb, w_rwkv, w_na, w_gate, mu)


def _block_diag(x, bd_mask):
    tiled = jnp.concatenate([x] * HEADS_PER_GROUP, axis=0)
    return jnp.where(bd_mask, tiled, jnp.zeros_like(tiled))


def _head_sums(xs, ones_blocks):
    rows, width = xs[0].shape
    gw = ones_blocks.shape[0]
    n_g = width // gw
    stacked = jnp.concatenate([x[:, i * gw:(i + 1) * gw] for x in xs for i in range(n_g)], axis=0)
    res = _dot(stacked.astype(BF16), ones_blocks)
    return [jnp.concatenate([res[(j * n_g + i) * rows:(j * n_g + i + 1) * rows] for i in range(n_g)],
                            axis=1) for j in range(len(xs))]


def _rwkv_kernel(us_ref, wdi_ref, wgate_ref, w0_ref, a0_ref, kk_ref, ka_ref, rk_ref, ones_ref, tri_ref,
                 o_ref, bonus_ref, *rest, reverse, width):
    g_ref = None if reverse else rest[0]
    state_ref = rest[-1]
    t = pl.program_id(1)
    bps, tb = us_ref.shape[0], us_ref.shape[1]
    rows_all = bps * tb
    c3 = 3 * width

    @pl.when(t == 0)
    def _():
        state_ref[...] = jnp.zeros_like(state_ref)

    us = us_ref[...].reshape(rows_all, us_ref.shape[2])
    r = us[:, 0:width]
    k = us[:, width:2 * width]
    v = us[:, 2 * width:c3]
    low = us[:, c3:c3 + 2 * LANES]

    lane = lax.broadcasted_iota(jnp.int32, (rows_all, LANES), 1)
    f_di = jnp.where(lane < 64, jnp.tanh(low[:, 0:LANES]), low[:, 0:LANES]).astype(BF16)
    lr = _dot(f_di, wdi_ref[...])
    z = w0_ref[...] + lr[:, 0:width]
    iclr = _sigmoid(a0_ref[...] + lr[:, width:2 * width])
    if g_ref is not None:
        g_ref[...] = (_dot(_sigmoid(low[:, LANES:]).astype(BF16), wgate_ref[...])
                      .astype(g_ref.dtype).reshape(g_ref.shape))

    logw = -math.exp(-0.5) * _sigmoid(z)

    kkr = k * kk_ref[...]
    kd = k * (1.0 + (iclr - 1.0) * ka_ref[...])
    ss, rkd = _head_sums([kkr * kkr, r * kd * rk_ref[...]], ones_ref[...])
    kk = kkr * lax.rsqrt(jnp.maximum(ss, 1e-24))
    a = -kk
    b = kk * iclr
    bonus_ref[...] = (rkd * v).astype(bonus_ref.dtype).reshape(bonus_ref.shape)

    hi = logw.astype(BF16)
    lo = (logw - hi.astype(F32)).astype(BF16)
    tri = tri_ref[...]
    cum = jnp.concatenate([_dot(tri, hi[bb * tb:(bb + 1) * tb]) + _dot(tri, lo[bb * tb:(bb + 1) * tb])
                           for bb in range(bps)], axis=0)

    e_incl = jnp.exp(cum)
    a_t = (a * jnp.exp(cum - logw)).astype(BF16)
    r_t = (r * e_incl).astype(BF16)
    e_neg = jnp.exp(-cum)
    b_t = (b * e_neg).astype(BF16)
    k_t = (kd * e_neg).astype(BF16)
    v_b = v.astype(BF16)

    shape_c = (CHUNK, GROUP_LANES)
    ti = lax.broadcasted_iota(jnp.int32, shape_c, 0)
    ji = lax.broadcasted_iota(jnp.int32, shape_c, 1) % CHUNK
    strict = (ji > ti) if reverse else (ji < ti)
    incl = (ji >= ti) if reverse else (ji <= ti)
    eye = jnp.where(ji == ti, 1.0, 0.0).astype(F32)
    same16 = (ti // 16) == (ji // 16)
    same32 = (ti // 32) == (ji // 32)
    lvl0 = strict & same16
    lvl1 = strict & same32 & jnp.logical_not(same16)
    lvl2 = strict & jnp.logical_not(same32)
    bi = lax.broadcasted_iota(jnp.int32, (GROUP_LANES, GROUP_LANES), 0) // HEAD_DIM
    bj = lax.broadcasted_iota(jnp.int32, (GROUP_LANES, GROUP_LANES), 1) // HEAD_DIM
    bd_mask = bi == bj

    def mm(x, y):
        return _dot(x.astype(BF16), _block_diag(y.astype(BF16), bd_mask))

    n_chunks = tb // CHUNK
    n_groups = width // GROUP_LANES
    order = range(n_chunks - 1, -1, -1) if reverse else range(n_chunks)
    chains = [(bb, c, gi) for bb in range(bps) for c in range(n_chunks) for gi in range(n_groups)]

    def part(arr, ch):
        bb, c, gi = ch
        r0 = bb * tb + c * CHUNK
        return arr[r0:r0 + CHUNK, gi * GROUP_LANES:(gi + 1) * GROUP_LANES]

    def each(fn, *dicts):
        return {ch: fn(*(d[ch] for d in dicts)) for ch in chains}

    bd = lambda y: _block_diag(y, bd_mask)
    bdf = lambda y: jnp.where(bd_mask, y, 0.0)

    at_c = {ch: part(a_t, ch) for ch in chains}
    rt_c = {ch: part(r_t, ch) for ch in chains}
    bk_c = {ch: jnp.concatenate([part(b_t, ch), part(k_t, ch)], axis=0) for ch in chains}
    v_c = {ch: part(v_b, ch) for ch in chains}
    ar = each(lambda p, q: jnp.concatenate([p, q], axis=0), at_c, rt_c)
    ab = each(lambda p, q: _dot_nt(p, bd(q[0:CHUNK])), ar, bk_c)
    ak = each(lambda p, q: _dot_nt(p, bd(q[CHUNK:])), ar, bk_c)
    a_rb = each(lambda p: jnp.where(incl, p[CHUNK:], 0.0).astype(BF16), ab)
    a_kk = each(lambda p: jnp.concatenate([jnp.where(strict, p[0:CHUNK], 0.0),
                                           jnp.where(incl, p[CHUNK:], 0.0)], axis=0).astype(BF16), ak)
    akv = each(lambda p, q: _dot(p, bd(q)), a_kk, v_c)
    n0 = each(lambda p: jnp.where(lvl0, p[0:CHUNK], 0.0), ab)
    pw = each(lambda p: mm(p, p), n0)
    x = each(lambda p: eye + p, n0)
    for _ in range(2):
        both = each(lambda p, q: mm(jnp.concatenate([p, q], axis=0), q), x, pw)
        x = each(lambda p, q: p + q[0:CHUNK], x, both)
        pw = each(lambda q: q[CHUNK:], both)
    x = each(lambda p, q: p + mm(p, q), x, pw)
    for lvl in (lvl1, lvl2):
        y = each(lambda p, q: mm(p, jnp.where(lvl, q[0:CHUNK], 0.0)), x, ab)
        x = each(lambda p, q: p + mm(q, p), x, y)
    tinv = each(lambda p: p.astype(BF16), x)
    wu = each(lambda p, a_, k_: _dot(p, jnp.concatenate([bd(a_), bd(k_[0:CHUNK].astype(BF16))], axis=1))
              .astype(BF16), tinv, at_c, akv)
    qo = each(lambda a_, w_: _dot(a_, jnp.concatenate([bd(w_[:, 0:GROUP_LANES]), bd(w_[:, GROUP_LANES:])],
                                                      axis=1)), a_rb, wu)
    q_c = each(lambda r_, p: (r_.astype(F32) + p[:, 0:GROUP_LANES]).astype(BF16), rt_c, qo)
    o_c = each(lambda p, k_: p[:, GROUP_LANES:] + k_[CHUNK:], qo, akv)
    zeros_c = jnp.zeros((CHUNK, GROUP_LANES), BF16)
    mc = each(lambda b_, w_, v_: _dot_tn(b_, jnp.concatenate(
        [w_, jnp.concatenate([zeros_c, v_], axis=1)], axis=0)), bk_c, wu, v_c)

    edges = [bb * tb + (c * CHUNK if reverse else (c + 1) * CHUNK - 1)
             for bb in range(bps) for c in range(n_chunks)]
    pad_rows = -len(edges) % 8
    edge_rows = jnp.concatenate([e_incl[e:e + 1, :] for e in edges]
                                + ([jnp.zeros((pad_rows, width), F32)] if pad_rows else []), axis=0)
    edge_cols = edge_rows.T
    p_c = {(bb, c, gi): edge_cols[gi * GROUP_LANES:(gi + 1) * GROUP_LANES,
                                  bb * n_chunks + c:bb * n_chunks + c + 1] for bb, c, gi in chains}
    m_c = each(lambda p, d: (bdf(p[:, 0:GROUP_LANES]) * d).astype(BF16), mc, p_c)
    c_c = each(lambda p, d: bdf(p[:, GROUP_LANES:]) * d, mc, p_c)

    lines = [(bb, gi) for bb in range(bps) for gi in range(n_groups)]
    st = {ln: state_ref[ln[0] * n_groups + ln[1]] for ln in lines}
    for c in order:
        for bb, gi in lines:
            ch = (bb, c, gi)
            both = _dot(jnp.concatenate([m_c[ch], q_c[ch]], axis=0), st[bb, gi].astype(BF16))
            o_ref[bb, c * CHUNK:(c + 1) * CHUNK, gi * GROUP_LANES:(gi + 1) * GROUP_LANES] = (
                both[GROUP_LANES:] + o_c[ch]).astype(o_ref.dtype)
            st[bb, gi] = st[bb, gi] * p_c[ch] + both[0:GROUP_LANES] + c_c[ch]
    for bb, gi in lines:
        state_ref[bb * n_groups + gi] = st[bb, gi]


def _rwkv_call(us3, wdi, wgate, w0, a0, k_k, k_a, r_k, ones_blocks, tri, *, reverse, width):
    bsz, s, cols = us3.shape
    tb = SCAN_BLOCK
    nt = s // tb

    def blk(t):
        return (nt - 1 - t) if reverse else t

    kern = functools.partial(_rwkv_kernel, reverse=reverse, width=width)
    const2 = lambda b, t: (0, 0)
    n_out = 2 if reverse else 3
    out_sds = jax.ShapeDtypeStruct((bsz, s, width), BF16)
    bps = SCAN_BATCH_ROWS if bsz % SCAN_BATCH_ROWS == 0 else 1
    out_spec = pl.BlockSpec((bps, tb, width), lambda b, t: (b, blk(t), 0))
    return pl.pallas_call(
        kern,
        grid=(bsz // bps, nt),
        in_specs=[pl.BlockSpec((bps, tb, cols), lambda b, t: (b, blk(t), 0)),
                  pl.BlockSpec(wdi.shape, const2),
                  pl.BlockSpec(wgate.shape, const2),
                  pl.BlockSpec((1, width), const2),
                  pl.BlockSpec((1, width), const2),
                  pl.BlockSpec((1, width), const2),
                  pl.BlockSpec((1, width), const2),
                  pl.BlockSpec((1, width), const2),
                  pl.BlockSpec(ones_blocks.shape, const2),
                  pl.BlockSpec(tri.shape, const2)],
        out_specs=[out_spec] * n_out,
        out_shape=[out_sds] * n_out,
        scratch_shapes=[pltpu.VMEM((bps * (width // GROUP_LANES), GROUP_LANES, GROUP_LANES), F32)],
        compiler_params=_cparams(("parallel", "arbitrary")),
        name="rwkv_bwd" if reverse else "rwkv_fwd",
    )(us3, wdi, wgate, w0, a0, k_k, k_a, r_k, ones_blocks, tri)


def _na_kernel(q_ref, k_ref, v_ref, bias_ref, o_ref, *, rows):
    g = pl.program_id(1)
    band = NA_BAND_ROWS * GRID_W
    start_row = jnp.clip(g * NA_ROWS_PER_STEP - NA_KH // 2, 0, rows - NA_BAND_ROWS)
    start = pl.multiple_of(start_row * GRID_W, GRID_W)
    n_pairs = q_ref.shape[2] // LANES
    tq = q_ref.shape[1]
    first_head = lax.broadcasted_iota(jnp.int32, (tq, LANES), 1) < HEAD_DIM
    ones_cols = jnp.ones((band, LANES), BF16)
    for bb, p in [(bb, p) for bb in range(q_ref.shape[0]) for p in range(n_pairs)]:
        ls = slice(p * LANES, (p + 1) * LANES)
        qp = q_ref[bb, :, ls]
        kp = k_ref[bb, pl.ds(start, band), ls]
        vp = v_ref[bb, pl.ds(start, band), ls]
        zero = jnp.zeros_like(qp)
        q2 = jnp.concatenate([jnp.where(first_head, qp, zero), jnp.where(first_head, zero, qp)], axis=0)
        sc = _dot_nt(q2, kp) + bias_ref[0, 2 * p:2 * p + 2].reshape(2 * tq, band)
        m = jnp.max(sc, axis=-1, keepdims=True)
        e = jnp.exp2((sc - m).astype(BF16))
        pv = _dot(e, jnp.concatenate([vp, ones_cols], axis=1))
        out = pv[:, 0:LANES] * (1.0 / pv[:, LANES:LANES + 1])
        o_ref[bb, :, ls] = jnp.where(first_head, out[0:tq], out[tq:]).astype(o_ref.dtype)


def _na_call(qkv3, bias, layer, width):
    bsz, s, _ = qkv3.shape
    rows = s // GRID_W
    tq = NA_ROWS_PER_STEP * GRID_W
    ng = rows // NA_ROWS_PER_STEP
    nh = width // HEAD_DIM

    def bias_idx(b, g):
        ty = jnp.where(g == 0, 0, jnp.where(g == ng - 1, 2, 1))
        return (ty, layer, 0, 0)

    bps = NA_BATCH_ROWS if bsz % NA_BATCH_ROWS == 0 else 1
    return pl.pallas_call(
        functools.partial(_na_kernel, rows=rows),
        grid=(bsz // bps, ng),
        in_specs=[pl.BlockSpec((bps, tq, width), lambda b, g: (b, g, 0)),
                  pl.BlockSpec((bps, s, width), lambda b, g: (b, 0, 1)),
                  pl.BlockSpec((bps, s, width), lambda b, g: (b, 0, 2)),
                  pl.BlockSpec((1, nh, tq, NA_BAND_ROWS * GRID_W), bias_idx)],
        out_specs=pl.BlockSpec((bps, tq, width), lambda b, g: (b, g, 0)),
        out_shape=jax.ShapeDtypeStruct((bsz, s, width), BF16),
        compiler_params=_cparams(("parallel", "arbitrary")),
        name="nbr_attn",
    )(qkv3, qkv3, qkv3, bias)


def _na_bias_tables(rpb, rows):
    ng = rows // NA_ROWS_PER_STEP
    kh = min(NA_KH, rows)
    nh, n_dr, n_dc = rpb.shape
    col = np.arange(GRID_W)
    cs = np.clip(col - NA_KW // 2, 0, GRID_W - NA_KW)
    ok_c = (col[None, :] >= cs[:, None]) & (col[None, :] < cs[:, None] + NA_KW)
    dc = col[None, :] - col[:, None] + (NA_KW - 1)
    onehot = (dc[None] == np.arange(n_dc)[:, None, None]) & ok_c[None]
    expand = jnp.asarray(onehot.reshape(n_dc, GRID_W * GRID_W), F32)
    by_dr = jnp.dot(rpb.reshape(nh * n_dr, n_dc), expand, precision=lax.Precision.HIGHEST)
    by_dr = jnp.where(ok_c, by_dr.reshape(nh, n_dr, GRID_W, GRID_W), NEG_BIG)
    outside = jnp.full((nh, GRID_W, GRID_W), NEG_BIG, F32)
    tables = []
    for g in (0, 1, ng - 1):
        bs = int(np.clip(g * NA_ROWS_PER_STEP - NA_KH // 2, 0, rows - NA_BAND_ROWS))
        q_rows = []
        for qi in range(g * NA_ROWS_PER_STEP, (g + 1) * NA_ROWS_PER_STEP):
            rs = int(np.clip(qi - kh // 2, 0, rows - kh))
            blocks = [by_dr[:, kr - qi + NA_KH - 1] if rs <= kr < rs + kh else outside
                      for kr in range(bs, bs + NA_BAND_ROWS)]
            q_rows.append(jnp.concatenate(blocks, axis=-1))
        tables.append(jnp.concatenate(q_rows, axis=1))
    return jnp.stack(tables) * LOG2E


def _merge_kernel(o0_ref, o1_ref, b0_ref, b1_ref, g_ref, yb_ref, gate_ref, x_ref, ones_ref,
                  gng_ref, gnb_ref, wa_ref, wb_ref, wo_ref, lg_ref, lb_ref, xo_ref, xob_ref, *,
                  alpha, d_model):
    ones_bd = ones_ref[...]
    inv_n = 1.0 / HEAD_DIM
    tm = x_ref.shape[0]
    parts = [slice(h * tm // ROW_SPLIT, (h + 1) * tm // ROW_SPLIT) for h in range(ROW_SPLIT)]
    o = [o0_ref[rs, :].astype(F32) + o1_ref[rs, :].astype(F32) for rs in parts]
    mean = [_dot(v.astype(BF16), ones_bd) * inv_n for v in o]
    oc = [v - mu for v, mu in zip(o, mean)]
    var = [_dot((v * v).astype(BF16), ones_bd) * inv_n for v in oc]
    on = [v * lax.rsqrt(s2 + GN_EPS) * gng_ref[...] + gnb_ref[...] for v, s2 in zip(oc, var)]
    ya = [(v + b0_ref[rs, :].astype(F32) + b1_ref[rs, :].astype(F32)) * g_ref[rs, :].astype(F32)
          for v, rs in zip(on, parts)]
    za = [_dot(v.astype(BF16), wa_ref[...]) for v in ya]
    zb = [_dot(yb_ref[rs, :], wb_ref[...]) for rs in parts]
    merged = [_sigmoid(gate_ref[rs, 0:d_model].astype(F32)) * a
              + _sigmoid(gate_ref[rs, d_model:].astype(F32)) * b for rs, a, b in zip(parts, za, zb)]
    y = [alpha * x_ref[rs, :] + _dot(v.astype(BF16), wo_ref[...]) for rs, v in zip(parts, merged)]
    for rs, yy in zip(parts, y):
        xn = _layer_norm_rows(yy, lg_ref[...], lb_ref[...])
        xo_ref[rs, :] = xn
        xob_ref[rs, :] = xn.astype(BF16)


def _merge_call(o0, o1, b0, b1, g, yb, gates, x, ones_bd, gn_g, gn_b, wa, wb, wo, ln_g, ln_b,
                alpha, tm):
    m, d = x.shape
    w = o0.shape[1]
    rowblk = lambda n: pl.BlockSpec((tm, n), lambda i: (i, 0))
    const = lambda shape: pl.BlockSpec(shape, lambda i: (0, 0))
    return pl.pallas_call(
        functools.partial(_merge_kernel, alpha=alpha, d_model=d),
        grid=(m // tm,),
        in_specs=[rowblk(w), rowblk(w), rowblk(w), rowblk(w), rowblk(w), rowblk(w),
                  rowblk(2 * d), rowblk(d), const(ones_bd.shape),
                  const((1, w)), const((1, w)), const(wa.shape), const(wb.shape), const(wo.shape),
                  const((1, d)), const((1, d))],
        out_specs=[rowblk(d), rowblk(d)],
        out_shape=[jax.ShapeDtypeStruct((m, d), F32), jax.ShapeDtypeStruct((m, d), BF16)],
        compiler_params=_cparams(("parallel",)),
        name="merge_out_ln",
    )(o0, o1, b0, b1, g, yb, gates, x, ones_bd, gn_g, gn_b, wa, wb, wo, ln_g, ln_b)


def _ffn_kernel(x_ref, xb_ref, wg_ref, wu_ref, wo_ref, lg_ref, lb_ref, xo_ref, xob_ref, *, alpha):
    xb = xb_ref[...]
    hg = _dot(xb, wg_ref[...])
    hu = _dot(xb, wu_ref[...])
    act = (hg * _sigmoid(hg) * hu).astype(BF16)
    y = alpha * x_ref[...] + _dot(act, wo_ref[...])
    xn = _layer_norm_rows(y, lg_ref[...], lb_ref[...])
    xo_ref[...] = xn
    xob_ref[...] = xn.astype(BF16)


def _ffn_call(x, xb, wg, wu, wo, ln_g, ln_b, alpha, tm):
    m, d = x.shape
    rowblk = pl.BlockSpec((tm, d), lambda i: (i, 0))
    const = lambda shape: pl.BlockSpec(shape, lambda i: (0, 0), pipeline_mode=pl.Buffered(1))
    return pl.pallas_call(
        functools.partial(_ffn_kernel, alpha=alpha),
        grid=(m // tm,),
        in_specs=[rowblk, rowblk, const(wg.shape), const(wu.shape), const(wo.shape),
                  pl.BlockSpec((1, d), lambda i: (0, 0)), pl.BlockSpec((1, d), lambda i: (0, 0))],
        out_specs=[rowblk, rowblk],
        out_shape=[jax.ShapeDtypeStruct((m, d), F32), jax.ShapeDtypeStruct((m, d), BF16)],
        compiler_params=_cparams(("parallel",)),
        name="swiglu_ln",
    )(x, xb, wg, wu, wo, ln_g, ln_b)


def _low_rank_weights(decay_up_d, iclr_up_d, gate_up, direction, width):
    r_dec, r_icl, r_gate = decay_up_d.shape[0], iclr_up_d.shape[0], gate_up.shape[0]
    w = jnp.zeros((LANES, 2 * width), F32)
    w = w.at[direction * r_dec:(direction + 1) * r_dec, 0:width].set(decay_up_d)
    w = w.at[64 + direction * r_icl:64 + (direction + 1) * r_icl, width:].set(iclr_up_d)
    wg = jnp.zeros((LANES, width), F32).at[0:r_gate].set(gate_up)
    return w.astype(BF16), wg.astype(BF16)


def kernel(x, ln_in_g, ln_in_b, w_in, shift_mu, decay_w0, decay_up, iclr_a0, iclr_up, gate_up,
           k_k, k_a, r_k, gn_g, gn_b, na_rpb, w_branch_rwkv, w_branch_na, w_out, ln1_g, ln1_b,
           w_ffn_in, w_ffn_out, ln2_g, ln2_b):
    bsz, s, d = x.shape
    depth = w_in.shape[0]
    width = k_k.shape[1]
    na_width = w_branch_na.shape[1]
    d_ff = w_ffn_out.shape[1]
    rwkv_cols = shift_mu.shape[2]
    rwkv_pad = 3 * width + 2 * LANES
    assert 2 * decay_up.shape[2] == 64 and 2 * iclr_up.shape[2] == 64 and gate_up.shape[1] <= LANES
    assert rwkv_cols == 3 * width + LANES + gate_up.shape[1] and s % SCAN_BLOCK == 0
    rows = s // GRID_W
    assert rows % NA_ROWS_PER_STEP == 0 and rows >= NA_BAND_ROWS and rows // NA_ROWS_PER_STEP >= 3
    alpha = (2.0 * depth) ** 0.25
    m = bsz * s
    tm = 512 if m % 512 == 0 else 256

    ones_bd = jnp.asarray(np.kron(np.eye(width // HEAD_DIM), np.ones((HEAD_DIM, HEAD_DIM))), BF16)
    ones_blocks = ones_bd[0:MXU_WIDTH, 0:MXU_WIDTH]
    pos = np.arange(SCAN_BLOCK)
    same_chunk = (pos[:, None] // CHUNK) == (pos[None, :] // CHUNK)
    tri_f = jnp.asarray(same_chunk & (pos[None, :] <= pos[:, None]), BF16)
    tri_r = jnp.asarray(same_chunk & (pos[None, :] >= pos[:, None]), BF16)

    na_bias = _na_bias_tables(na_rpb.reshape((-1,) + na_rpb.shape[2:]), rows)

    x2, xb = _ln_call(x.reshape(m, d), ln_in_g, ln_in_b, tm)
    row = lambda p: p.reshape(1, -1)
    for l in range(depth):
        wl = w_in[l]
        w_r = jnp.pad(wl[:, :rwkv_cols], ((0, 0), (0, rwkv_pad - rwkv_cols))).astype(BF16)
        w_n = wl[:, rwkv_cols:rwkv_cols + 3 * na_width]
        w_n = jnp.concatenate([w_n[:, :na_width] * (HEAD_DIM ** -0.5 * LOG2E), w_n[:, na_width:]],
                              axis=1).astype(BF16)
        w_g = wl[:, rwkv_cols + 3 * na_width:].astype(BF16)
        mu = jnp.pad(shift_mu[l], ((0, 0), (0, rwkv_pad - rwkv_cols)))

        us, qkv, gates = _proj_call(xb, w_r, w_n, w_g, mu, tm, s)
        us3 = us.reshape(bsz, s, rwkv_pad)

        outs = []
        for direction, tri in ((0, tri_f), (1, tri_r)):
            wdi, wgate = _low_rank_weights(decay_up[l, direction], iclr_up[l, direction], gate_up[l],
                                           direction, width)
            outs.append(_rwkv_call(
                us3, wdi, wgate, row(decay_w0[l, direction]), row(iclr_a0[l, direction]),
                row(k_k[l]), row(k_a[l]), row(r_k[l]), ones_blocks, tri,
                reverse=bool(direction), width=width))
        (o0, b0, g0), (o1, b1) = outs

        yb = _na_call(qkv.reshape(bsz, s, 3 * na_width), na_bias, l, na_width)

        flat = lambda t: t.reshape(m, -1)
        x2, xb = _merge_call(
            flat(o0), flat(o1), flat(b0), flat(b1), flat(g0), flat(yb), gates, x2, ones_bd,
            row(gn_g[l]), row(gn_b[l]), w_branch_rwkv[l].astype(BF16), w_branch_na[l].astype(BF16),
            w_out[l].astype(BF16), row(ln1_g[l]), row(ln1_b[l]), alpha, tm)

        x2, xb = _ffn_call(
            x2, xb, w_ffn_in[l, :, :d_ff].astype(BF16), w_ffn_in[l, :, d_ff:].astype(BF16),
            w_ffn_out[l].astype(BF16), row(ln2_g[l]), row(ln2_b[l]), alpha, tm)
    return x2.reshape(bsz, s, d)
```

```python
import functools
import math

import jax
import jax.numpy as jnp
import numpy as np
from jax import lax
from jax.experimental import pallas as pl
from jax.experimental.pallas import tpu as pltpu

F32 = jnp.float32
BF16 = jnp.bfloat16

HEAD_DIM = 64
GRID_W = 64
NA_KH = 8
NA_KW = 16
NA_ROWS_PER_STEP = 4
NA_BAND_ROWS = NA_KH + NA_ROWS_PER_STEP - 1
NA_BATCH_ROWS = 2
LN_EPS = 1e-5
GN_EPS = 64e-5
NEG_BIG = -1e30
LOG2E = math.log2(math.e)
LANES = 128
MXU_WIDTH = 256
CHUNK = 64
SCAN_BLOCK = 256
SCAN_BATCH_ROWS = 4
HEADS_PER_GROUP = 2
GROUP_LANES = HEADS_PER_GROUP * HEAD_DIM
ROW_SPLIT = 2
VMEM_LIMIT = 56 * 1024 * 1024


def _cparams(sem):
    return pltpu.CompilerParams(dimension_semantics=sem, vmem_limit_bytes=VMEM_LIMIT)


def _dot(a, b):
    return jnp.dot(a, b, preferred_element_type=F32)


def _dot_nt(a, b):
    return lax.dot_general(a, b, (((1,), (1,)), ((), ())), preferred_element_type=F32)


def _dot_tn(a, b):
    return lax.dot_general(a, b, (((0,), (0,)), ((), ())), preferred_element_type=F32)


def _layer_norm_rows(x, g, b):
    mean = jnp.mean(x, axis=-1, keepdims=True)
    xc = x - mean
    var = jnp.mean(xc * xc, axis=-1, keepdims=True)
    return xc * lax.rsqrt(var + LN_EPS) * g + b


def _sigmoid(x):
    return 1.0 / (1.0 + jnp.exp(-x))


def _ln_kernel(x_ref, g_ref, b_ref, o_ref, ob_ref):
    y = _layer_norm_rows(x_ref[...], g_ref[...], b_ref[...])
    o_ref[...] = y
    ob_ref[...] = y.astype(BF16)


def _ln_call(x, g, b, tm):
    m, d = x.shape
    return pl.pallas_call(
        _ln_kernel,
        grid=(m // tm,),
        in_specs=[pl.BlockSpec((tm, d), lambda i: (i, 0)),
                  pl.BlockSpec((1, d), lambda i: (0, 0)),
                  pl.BlockSpec((1, d), lambda i: (0, 0))],
        out_specs=[pl.BlockSpec((tm, d), lambda i: (i, 0)),
                   pl.BlockSpec((tm, d), lambda i: (i, 0))],
        out_shape=[jax.ShapeDtypeStruct((m, d), F32), jax.ShapeDtypeStruct((m, d), BF16)],
        compiler_params=_cparams(("parallel",)),
        name="input_ln",
    )(x, g.reshape(1, d), b.reshape(1, d))


def _proj_kernel(x_ref, xp_ref, xn_ref, wr_ref, wn_ref, wg_ref, mu_ref, us_ref, qkv_ref, gate_ref, *,
                 blocks_per_seq):
    i = pl.program_id(0)
    x = x_ref[...]
    tm = x.shape[0]
    h = xp_ref.shape[0]
    xp = jnp.where(i % blocks_per_seq == 0, jnp.zeros_like(xp_ref[...]), xp_ref[...])
    xn = jnp.where(i % blocks_per_seq == blocks_per_seq - 1, jnp.zeros_like(xn_ref[...]), xn_ref[...])
    ext = _dot(jnp.concatenate([xp, x, xn], axis=0), wr_ref[...])
    keep = 1.0 - mu_ref[0:1, :] - mu_ref[1:2, :]
    us_ref[...] = (keep * ext[h:h + tm] + mu_ref[0:1, :] * ext[h - 1:h - 1 + tm]
                   + mu_ref[1:2, :] * ext[h + 1:h + 1 + tm])
    qkv_ref[...] = _dot(x, wn_ref[...]).astype(BF16)
    gate_ref[...] = _dot(x, wg_ref[...]).astype(BF16)


def _proj_call(xb, w_rwkv, w_na, w_gate, mu, tm, seq):
    m, d = xb.shape
    nr, nn, ng = w_rwkv.shape[1], w_na.shape[1], w_gate.shape[1]
    const = lambda i: (0, 0)
    halo = 16
    hb = tm // halo
    last = m // halo - 1
    return pl.pallas_call(
        functools.partial(_proj_kernel, blocks_per_seq=seq // tm),
        grid=(m // tm,),
        in_specs=[pl.BlockSpec((tm, d), lambda i: (i, 0)),
                  pl.BlockSpec((halo, d), lambda i: (jnp.maximum(i * hb - 1, 0), 0)),
                  pl.BlockSpec((halo, d), lambda i: (jnp.minimum((i + 1) * hb, last), 0)),
                  pl.BlockSpec((d, nr), const),
                  pl.BlockSpec((d, nn), const),
                  pl.BlockSpec((d, ng), const),
                  pl.BlockSpec(mu.shape, const)],
        out_specs=[pl.BlockSpec((tm, nr), lambda i: (i, 0)),
                   pl.BlockSpec((tm, nn), lambda i: (i, 0)),
                   pl.BlockSpec((tm, ng), lambda i: (i, 0))],
        out_shape=[jax.ShapeDtypeStruct((m, nr), F32),
                   jax.ShapeDtypeStruct((m, nn), BF16),
                   jax.ShapeDtypeStruct((m, ng), BF16)],
        compiler_params=_cparams(("parallel",)),
        name="in_proj",
    )(xb, xb, xb, w_rwkv, w_na, w_gate, mu)


def _block_diag(x, bd_mask):
    tiled = jnp.concatenate([x] * HEADS_PER_GROUP, axis=0)
    return jnp.where(bd_mask, tiled, jnp.zeros_like(tiled))


def _head_sums(xs, ones_blocks):
    rows, width = xs[0].shape
    gw = ones_blocks.shape[0]
    n_g = width // gw
    stacked = jnp.concatenate([x[:, i * gw:(i + 1) * gw] for x in xs for i in range(n_g)], axis=0)
    res = _dot(stacked.astype(BF16), ones_blocks)
    return [jnp.concatenate([res[(j * n_g + i) * rows:(j * n_g + i + 1) * rows] for i in range(n_g)],
                            axis=1) for j in range(len(xs))]


def _rwkv_kernel(us_ref, wdi_ref, wgate_ref, w0_ref, a0_ref, kk_ref, ka_ref, rk_ref, ones_ref, tri_ref,
                 o_ref, bonus_ref, *rest, reverse, width):
    g_ref = None if reverse else rest[0]
    state_ref = rest[-1]
    t = pl.program_id(1)
    bps, tb = us_ref.shape[0], us_ref.shape[1]
    rows_all = bps * tb
    c3 = 3 * width

    @pl.when(t == 0)
    def _():
        state_ref[...] = jnp.zeros_like(state_ref)

    us = us_ref[...].reshape(rows_all, us_ref.shape[2])
    r = us[:, 0:width]
    k = us[:, width:2 * width]
    v = us[:, 2 * width:c3]
    low = us[:, c3:c3 + 2 * LANES]

    lane = lax.broadcasted_iota(jnp.int32, (rows_all, LANES), 1)
    f_di = jnp.where(lane < 64, jnp.tanh(low[:, 0:LANES]), low[:, 0:LANES]).astype(BF16)
    lr = _dot(f_di, wdi_ref[...])
    z = w0_ref[...] + lr[:, 0:width]
    iclr = _sigmoid(a0_ref[...] + lr[:, width:2 * width])
    if g_ref is not None:
        g_ref[...] = (_dot(_sigmoid(low[:, LANES:]).astype(BF16), wgate_ref[...])
                      .astype(g_ref.dtype).reshape(g_ref.shape))

    logw = -math.exp(-0.5) * _sigmoid(z)

    kkr = k * kk_ref[...]
    kd = k * (1.0 + (iclr - 1.0) * ka_ref[...])
    ss, rkd = _head_sums([kkr * kkr, r * kd * rk_ref[...]], ones_ref[...])
    kk = kkr * lax.rsqrt(jnp.maximum(ss, 1e-24))
    a = -kk
    b = kk * iclr
    bonus_ref[...] = (rkd * v).astype(bonus_ref.dtype).reshape(bonus_ref.shape)

    hi = logw.astype(BF16)
    lo = (logw - hi.astype(F32)).astype(BF16)
    tri = tri_ref[...]
    cum = jnp.concatenate([_dot(tri, hi[bb * tb:(bb + 1) * tb]) + _dot(tri, lo[bb * tb:(bb + 1) * tb])
                           for bb in range(bps)], axis=0)

    e_incl = jnp.exp(cum)
    a_t = (a * jnp.exp(cum - logw)).astype(BF16)
    r_t = (r * e_incl).astype(BF16)
    e_neg = jnp.exp(-cum)
    b_t = (b * e_neg).astype(BF16)
    k_t = (kd * e_neg).astype(BF16)
    v_b = v.astype(BF16)

    shape_c = (CHUNK, GROUP_LANES)
    ti = lax.broadcasted_iota(jnp.int32, shape_c, 0)
    ji = lax.broadcasted_iota(jnp.int32, shape_c, 1) % CHUNK
    strict = (ji > ti) if reverse else (ji < ti)
    incl = (ji >= ti) if reverse else (ji <= ti)
    eye = jnp.where(ji == ti, 1.0, 0.0).astype(F32)
    same16 = (ti // 16) == (ji // 16)
    same32 = (ti // 32) == (ji // 32)
    lvl0 = strict & same16
    lvl1 = strict & same32 & jnp.logical_not(same16)
    lvl2 = strict & jnp.logical_not(same32)
    bi = lax.broadcasted_iota(jnp.int32, (GROUP_LANES, GROUP_LANES), 0) // HEAD_DIM
    bj = lax.broadcasted_iota(jnp.int32, (GROUP_LANES, GROUP_LANES), 1) // HEAD_DIM
    bd_mask = bi == bj

    def mm(x, y):
        return _dot(x.astype(BF16), _block_diag(y.astype(BF16), bd_mask))

    n_chunks = tb // CHUNK
    n_groups = width // GROUP_LANES
    order = range(n_chunks - 1, -1, -1) if reverse else range(n_chunks)
    chains = [(bb, c, gi) for bb in range(bps) for c in range(n_chunks) for gi in range(n_groups)]

    def part(arr, ch):
        bb, c, gi = ch
        r0 = bb * tb + c * CHUNK
        return arr[r0:r0 + CHUNK, gi * GROUP_LANES:(gi + 1) * GROUP_LANES]

    def each(fn, *dicts):
        return {ch: fn(*(d[ch] for d in dicts)) for ch in chains}

    bd = lambda y: _block_diag(y, bd_mask)
    bdf = lambda y: jnp.where(bd_mask, y, 0.0)

    at_c = {ch: part(a_t, ch) for ch in chains}
    rt_c = {ch: part(r_t, ch) for ch in chains}
    bk_c = {ch: jnp.concatenate([part(b_t, ch), part(k_t, ch)], axis=0) for ch in chains}
    v_c = {ch: part(v_b, ch) for ch in chains}
    ar = each(lambda p, q: jnp.concatenate([p, q], axis=0), at_c, rt_c)
    ab = each(lambda p, q: _dot_nt(p, bd(q[0:CHUNK])), ar, bk_c)
    ak = each(lambda p, q: _dot_nt(p, bd(q[CHUNK:])), ar, bk_c)
    a_rb = each(lambda p: jnp.where(incl, p[CHUNK:], 0.0).astype(BF16), ab)
    a_kk = each(lambda p: jnp.concatenate([jnp.where(strict, p[0:CHUNK], 0.0),
                                           jnp.where(incl, p[CHUNK:], 0.0)], axis=0).astype(BF16), ak)
    akv = each(lambda p, q: _dot(p, bd(q)), a_kk, v_c)
    n0 = each(lambda p: jnp.where(lvl0, p[0:CHUNK], 0.0), ab)
    pw = each(lambda p: mm(p, p), n0)
    x = each(lambda p: eye + p, n0)
    for _ in range(2):
        both = each(lambda p, q: mm(jnp.concatenate([p, q], axis=0), q), x, pw)
        x = each(lambda p, q: p + q[0:CHUNK], x, both)
        pw = each(lambda q: q[CHUNK:], both)
    x = each(lambda p, q: p + mm(p, q), x, pw)
    for lvl in (lvl1, lvl2):
        y = each(lambda p, q: mm(p, jnp.where(lvl, q[0:CHUNK], 0.0)), x, ab)
        x = each(lambda p, q: p + mm(q, p), x, y)
    tinv = each(lambda p: p.astype(BF16), x)
    wu = each(lambda p, a_, k_: _dot(p, jnp.concatenate([bd(a_), bd(k_[0:CHUNK].astype(BF16))], axis=1))
              .astype(BF16), tinv, at_c, akv)
    qo = each(lambda a_, w_: _dot(a_, jnp.concatenate([bd(w_[:, 0:GROUP_LANES]), bd(w_[:, GROUP_LANES:])],
                                                      axis=1)), a_rb, wu)
    q_c = each(lambda r_, p: (r_.astype(F32) + p[:, 0:GROUP_LANES]).astype(BF16), rt_c, qo)
    o_c = each(lambda p, k_: p[:, GROUP_LANES:] + k_[CHUNK:], qo, akv)
    zeros_c = jnp.zeros((CHUNK, GROUP_LANES), BF16)
    mc = each(lambda b_, w_, v_: _dot_tn(b_, jnp.concatenate(
        [w_, jnp.concatenate([zeros_c, v_], axis=1)], axis=0)), bk_c, wu, v_c)

    edges = [bb * tb + (c * CHUNK if reverse else (c + 1) * CHUNK - 1)
             for bb in range(bps) for c in range(n_chunks)]
    pad_rows = -len(edges) % 8
    edge_rows = jnp.concatenate([e_incl[e:e + 1, :] for e in edges]
                                + ([jnp.zeros((pad_rows, width), F32)] if pad_rows else []), axis=0)
    edge_cols = edge_rows.T
    p_c = {(bb, c, gi): edge_cols[gi * GROUP_LANES:(gi + 1) * GROUP_LANES,
                                  bb * n_chunks + c:bb * n_chunks + c + 1] for bb, c, gi in chains}
    m_c = each(lambda p, d: (bdf(p[:, 0:GROUP_LANES]) * d).astype(BF16), mc, p_c)
    c_c = each(lambda p, d: bdf(p[:, GROUP_LANES:]) * d, mc, p_c)

    lines = [(bb, gi) for bb in range(bps) for gi in range(n_groups)]
    st = {ln: state_ref[ln[0] * n_groups + ln[1]] for ln in lines}
    for c in order:
        for bb, gi in lines:
            ch = (bb, c, gi)
            both = _dot(jnp.concatenate([m_c[ch], q_c[ch]], axis=0), st[bb, gi].astype(BF16))
            o_ref[bb, c * CHUNK:(c + 1) * CHUNK, gi * GROUP_LANES:(gi + 1) * GROUP_LANES] = (
                both[GROUP_LANES:] + o_c[ch]).astype(o_ref.dtype)
            st[bb, gi] = st[bb, gi] * p_c[ch] + both[0:GROUP_LANES] + c_c[ch]
    for bb, gi in lines:
        state_ref[bb * n_groups + gi] = st[bb, gi]


def _rwkv_call(us3, wdi, wgate, w0, a0, k_k, k_a, r_k, ones_blocks, tri, *, reverse, width):
    bsz, s, cols = us3.shape
    tb = SCAN_BLOCK
    nt = s // tb

    def blk(t):
        return (nt - 1 - t) if reverse else t

    kern = functools.partial(_rwkv_kernel, reverse=reverse, width=width)
    const2 = lambda b, t: (0, 0)
    n_out = 2 if reverse else 3
    out_sds = jax.ShapeDtypeStruct((bsz, s, width), BF16)
    bps = SCAN_BATCH_ROWS if bsz % SCAN_BATCH_ROWS == 0 else 1
    out_spec = pl.BlockSpec((bps, tb, width), lambda b, t: (b, blk(t), 0))
    return pl.pallas_call(
        kern,
        grid=(bsz // bps, nt),
        in_specs=[pl.BlockSpec((bps, tb, cols), lambda b, t: (b, blk(t), 0)),
                  pl.BlockSpec(wdi.shape, const2),
                  pl.BlockSpec(wgate.shape, const2),
                  pl.BlockSpec((1, width), const2),
                  pl.BlockSpec((1, width), const2),
                  pl.BlockSpec((1, width), const2),
                  pl.BlockSpec((1, width), const2),
                  pl.BlockSpec((1, width), const2),
                  pl.BlockSpec(ones_blocks.shape, const2),
                  pl.BlockSpec(tri.shape, const2)],
        out_specs=[out_spec] * n_out,
        out_shape=[out_sds] * n_out,
        scratch_shapes=[pltpu.VMEM((bps * (width // GROUP_LANES), GROUP_LANES, GROUP_LANES), F32)],
        compiler_params=_cparams(("parallel", "arbitrary")),
        name="rwkv_bwd" if reverse else "rwkv_fwd",
    )(us3, wdi, wgate, w0, a0, k_k, k_a, r_k, ones_blocks, tri)


def _na_kernel(q_ref, k_ref, v_ref, bias_ref, o_ref, *, rows):
    g = pl.program_id(1)
    band = NA_BAND_ROWS * GRID_W
    start_row = jnp.clip(g * NA_ROWS_PER_STEP - NA_KH // 2, 0, rows - NA_BAND_ROWS)
    start = pl.multiple_of(start_row * GRID_W, GRID_W)
    n_pairs = q_ref.shape[2] // LANES
    tq = q_ref.shape[1]
    first_head = lax.broadcasted_iota(jnp.int32, (tq, LANES), 1) < HEAD_DIM
    ones_cols = jnp.ones((band, LANES), BF16)
    for bb, p in [(bb, p) for bb in range(q_ref.shape[0]) for p in range(n_pairs)]:
        ls = slice(p * LANES, (p + 1) * LANES)
        qp = q_ref[bb, :, ls]
        kp = k_ref[bb, pl.ds(start, band), ls]
        vp = v_ref[bb, pl.ds(start, band), ls]
        zero = jnp.zeros_like(qp)
        q2 = jnp.concatenate([jnp.where(first_head, qp, zero), jnp.where(first_head, zero, qp)], axis=0)
        sc = _dot_nt(q2, kp) + bias_ref[0, 2 * p:2 * p + 2].reshape(2 * tq, band)
        m = jnp.max(sc, axis=-1, keepdims=True)
        e = jnp.exp2((sc - m).astype(BF16))
        pv = _dot(e, jnp.concatenate([vp, ones_cols], axis=1))
        out = pv[:, 0:LANES] * (1.0 / pv[:, LANES:LANES + 1])
        o_ref[bb, :, ls] = jnp.where(first_head, out[0:tq], out[tq:]).astype(o_ref.dtype)


def _na_call(qkv3, bias, layer, width):
    bsz, s, _ = qkv3.shape
    rows = s // GRID_W
    tq = NA_ROWS_PER_STEP * GRID_W
    ng = rows // NA_ROWS_PER_STEP
    nh = width // HEAD_DIM

    def bias_idx(b, g):
        ty = jnp.where(g == 0, 0, jnp.where(g == ng - 1, 2, 1))
        return (ty, layer, 0, 0)

    bps = NA_BATCH_ROWS if bsz % NA_BATCH_ROWS == 0 else 1
    return pl.pallas_call(
        functools.partial(_na_kernel, rows=rows),
        grid=(bsz // bps, ng),
        in_specs=[pl.BlockSpec((bps, tq, width), lambda b, g: (b, g, 0)),
                  pl.BlockSpec((bps, s, width), lambda b, g: (b, 0, 1)),
                  pl.BlockSpec((bps, s, width), lambda b, g: (b, 0, 2)),
                  pl.BlockSpec((1, nh, tq, NA_BAND_ROWS * GRID_W), bias_idx)],
        out_specs=pl.BlockSpec((bps, tq, width), lambda b, g: (b, g, 0)),
        out_shape=jax.ShapeDtypeStruct((bsz, s, width), BF16),
        compiler_params=_cparams(("parallel", "arbitrary")),
        name="nbr_attn",
    )(qkv3, qkv3, qkv3, bias)


def _na_bias_tables(rpb, rows):
    ng = rows // NA_ROWS_PER_STEP
    kh = min(NA_KH, rows)
    nh, n_dr, n_dc = rpb.shape
    col = np.arange(GRID_W)
    cs = np.clip(col - NA_KW // 2, 0, GRID_W - NA_KW)
    ok_c = (col[None, :] >= cs[:, None]) & (col[None, :] < cs[:, None] + NA_KW)
    dc = col[None, :] - col[:, None] + (NA_KW - 1)
    onehot = (dc[None] == np.arange(n_dc)[:, None, None]) & ok_c[None]
    expand = jnp.asarray(onehot.reshape(n_dc, GRID_W * GRID_W), F32)
    by_dr = jnp.dot(rpb.reshape(nh * n_dr, n_dc), expand, precision=lax.Precision.HIGHEST)
    by_dr = jnp.where(ok_c, by_dr.reshape(nh, n_dr, GRID_W, GRID_W), NEG_BIG)
    outside = jnp.full((nh, GRID_W, GRID_W), NEG_BIG, F32)
    tables = []
    for g in (0, 1, ng - 1):
        bs = int(np.clip(g * NA_ROWS_PER_STEP - NA_KH // 2, 0, rows - NA_BAND_ROWS))
        q_rows = []
        for qi in range(g * NA_ROWS_PER_STEP, (g + 1) * NA_ROWS_PER_STEP):
            rs = int(np.clip(qi - kh // 2, 0, rows - kh))
            blocks = [by_dr[:, kr - qi + NA_KH - 1] if rs <= kr < rs + kh else outside
                      for kr in range(bs, bs + NA_BAND_ROWS)]
            q_rows.append(jnp.concatenate(blocks, axis=-1))
        tables.append(jnp.concatenate(q_rows, axis=1))
    return jnp.stack(tables) * LOG2E


def _merge_kernel(o0_ref, o1_ref, b0_ref, b1_ref, g_ref, yb_ref, gate_ref, x_ref, ones_ref,
                  gng_ref, gnb_ref, wa_ref, wb_ref, wo_ref, lg_ref, lb_ref, xo_ref, xob_ref, *,
                  alpha, d_model):
    ones_bd = ones_ref[...]
    inv_n = 1.0 / HEAD_DIM
    tm = x_ref.shape[0]
    parts = [slice(h * tm // ROW_SPLIT, (h + 1) * tm // ROW_SPLIT) for h in range(ROW_SPLIT)]
    o = [o0_ref[rs, :].astype(F32) + o1_ref[rs, :].astype(F32) for rs in parts]
    mean = [_dot(v.astype(BF16), ones_bd) * inv_n for v in o]
    oc = [v - mu for v, mu in zip(o, mean)]
    var = [_dot((v * v).astype(BF16), ones_bd) * inv_n for v in oc]
    on = [v * lax.rsqrt(s2 + GN_EPS) * gng_ref[...] + gnb_ref[...] for v, s2 in zip(oc, var)]
    ya = [(v + b0_ref[rs, :].astype(F32) + b1_ref[rs, :].astype(F32)) * g_ref[rs, :].astype(F32)
          for v, rs in zip(on, parts)]
    za = [_dot(v.astype(BF16), wa_ref[...]) for v in ya]
    zb = [_dot(yb_ref[rs, :], wb_ref[...]) for rs in parts]
    merged = [_sigmoid(gate_ref[rs, 0:d_model].astype(F32)) * a
              + _sigmoid(gate_ref[rs, d_model:].astype(F32)) * b for rs, a, b in zip(parts, za, zb)]
    y = [alpha * x_ref[rs, :] + _dot(v.astype(BF16), wo_ref[...]) for rs, v in zip(parts, merged)]
    for rs, yy in zip(parts, y):
        xn = _layer_norm_rows(yy, lg_ref[...], lb_ref[...])
        xo_ref[rs, :] = xn
        xob_ref[rs, :] = xn.astype(BF16)


def _merge_call(o0, o1, b0, b1, g, yb, gates, x, ones_bd, gn_g, gn_b, wa, wb, wo, ln_g, ln_b,
                alpha, tm):
    m, d = x.shape
    w = o0.shape[1]
    rowblk = lambda n: pl.BlockSpec((tm, n), lambda i: (i, 0))
    const = lambda shape: pl.BlockSpec(shape, lambda i: (0, 0))
    return pl.pallas_call(
        functools.partial(_merge_kernel, alpha=alpha, d_model=d),
        grid=(m // tm,),
        in_specs=[rowblk(w), rowblk(w), rowblk(w), rowblk(w), rowblk(w), rowblk(w),
                  rowblk(2 * d), rowblk(d), const(ones_bd.shape),
                  const((1, w)), const((1, w)), const(wa.shape), const(wb.shape), const(wo.shape),
                  const((1, d)), const((1, d))],
        out_specs=[rowblk(d), rowblk(d)],
        out_shape=[jax.ShapeDtypeStruct((m, d), F32), jax.ShapeDtypeStruct((m, d), BF16)],
        compiler_params=_cparams(("parallel",)),
        name="merge_out_ln",
    )(o0, o1, b0, b1, g, yb, gates, x, ones_bd, gn_g, gn_b, wa, wb, wo, ln_g, ln_b)


def _ffn_kernel(x_ref, xb_ref, wg_ref, wu_ref, wo_ref, lg_ref, lb_ref, xo_ref, xob_ref, *, alpha):
    xb = xb_ref[...]
    hg = _dot(xb, wg_ref[...])
    hu = _dot(xb, wu_ref[...])
    act = (hg * _sigmoid(hg) * hu).astype(BF16)
    y = alpha * x_ref[...] + _dot(act, wo_ref[...])
    xn = _layer_norm_rows(y, lg_ref[...], lb_ref[...])
    xo_ref[...] = xn
    xob_ref[...] = xn.astype(BF16)


def _ffn_call(x, xb, wg, wu, wo, ln_g, ln_b, alpha, tm):
    m, d = x.shape
    rowblk = pl.BlockSpec((tm, d), lambda i: (i, 0))
    const = lambda shape: pl.BlockSpec(shape, lambda i: (0, 0), pipeline_mode=pl.Buffered(1))
    return pl.pallas_call(
        functools.partial(_ffn_kernel, alpha=alpha),
        grid=(m // tm,),
        in_specs=[rowblk, rowblk, const(wg.shape), const(wu.shape), const(wo.shape),
                  pl.BlockSpec((1, d), lambda i: (0, 0)), pl.BlockSpec((1, d), lambda i: (0, 0))],
        out_specs=[rowblk, rowblk],
        out_shape=[jax.ShapeDtypeStruct((m, d), F32), jax.ShapeDtypeStruct((m, d), BF16)],
        compiler_params=_cparams(("parallel",)),
        name="swiglu_ln",
    )(x, xb, wg, wu, wo, ln_g, ln_b)


def _low_rank_weights(decay_up_d, iclr_up_d, gate_up, direction, width):
    r_dec, r_icl, r_gate = decay_up_d.shape[0], iclr_up_d.shape[0], gate_up.shape[0]
    w = jnp.zeros((LANES, 2 * width), F32)
    w = w.at[direction * r_dec:(direction + 1) * r_dec, 0:width].set(decay_up_d)
    w = w.at[64 + direction * r_icl:64 + (direction + 1) * r_icl, width:].set(iclr_up_d)
    wg = jnp.zeros((LANES, width), F32).at[0:r_gate].set(gate_up)
    return w.astype(BF16), wg.astype(BF16)


def kernel(x, ln_in_g, ln_in_b, w_in, shift_mu, decay_w0, decay_up, iclr_a0, iclr_up, gate_up,
           k_k, k_a, r_k, gn_g, gn_b, na_rpb, w_branch_rwkv, w_branch_na, w_out, ln1_g, ln1_b,
           w_ffn_in, w_ffn_out, ln2_g, ln2_b):
    bsz, s, d = x.shape
    depth = w_in.shape[0]
    width = k_k.shape[1]
    na_width = w_branch_na.shape[1]
    d_ff = w_ffn_out.shape[1]
    rwkv_cols = shift_mu.shape[2]
    rwkv_pad = 3 * width + 2 * LANES
    assert 2 * decay_up.shape[2] == 64 and 2 * iclr_up.shape[2] == 64 and gate_up.shape[1] <= LANES
    assert rwkv_cols == 3 * width + LANES + gate_up.shape[1] and s % SCAN_BLOCK == 0
    rows = s // GRID_W
    assert rows % NA_ROWS_PER_STEP == 0 and rows >= NA_BAND_ROWS and rows // NA_ROWS_PER_STEP >= 3
    alpha = (2.0 * depth) ** 0.25
    m = bsz * s
    tm = 512 if m % 512 == 0 else 256

    ones_bd = jnp.asarray(np.kron(np.eye(width // HEAD_DIM), np.ones((HEAD_DIM, HEAD_DIM))), BF16)
    ones_blocks = ones_bd[0:MXU_WIDTH, 0:MXU_WIDTH]
    pos = np.arange(SCAN_BLOCK)
    same_chunk = (pos[:, None] // CHUNK) == (pos[None, :] // CHUNK)
    tri_f = jnp.asarray(same_chunk & (pos[None, :] <= pos[:, None]), BF16)
    tri_r = jnp.asarray(same_chunk & (pos[None, :] >= pos[:, None]), BF16)

    na_bias = _na_bias_tables(na_rpb.reshape((-1,) + na_rpb.shape[2:]), rows)

    x2, xb = _ln_call(x.reshape(m, d), ln_in_g, ln_in_b, tm)
    row = lambda p: p.reshape(1, -1)
    for l in range(depth):
        wl = w_in[l]
        w_r = jnp.pad(wl[:, :rwkv_cols], ((0, 0), (0, rwkv_pad - rwkv_cols))).astype(BF16)
        w_n = wl[:, rwkv_cols:rwkv_cols + 3 * na_width]
        w_n = jnp.concatenate([w_n[:, :na_width] * (HEAD_DIM ** -0.5 * LOG2E), w_n[:, na_width:]],
                              axis=1).astype(BF16)
        w_g = wl[:, rwkv_cols + 3 * na_width:].astype(BF16)
        mu = jnp.pad(shift_mu[l], ((0, 0), (0, rwkv_pad - rwkv_cols)))

        us, qkv, gates = _proj_call(xb, w_r, w_n, w_g, mu, tm, s)
        us3 = us.reshape(bsz, s, rwkv_pad)

        outs = []
        for direction, tri in ((0, tri_f), (1, tri_r)):
            wdi, wgate = _low_rank_weights(decay_up[l, direction], iclr_up[l, direction], gate_up[l],
                                           direction, width)
            outs.append(_rwkv_call(
                us3, wdi, wgate, row(decay_w0[l, direction]), row(iclr_a0[l, direction]),
                row(k_k[l]), row(k_a[l]), row(r_k[l]), ones_blocks, tri,
                reverse=bool(direction), width=width))
        (o0, b0, g0), (o1, b1) = outs

        yb = _na_call(qkv.reshape(bsz, s, 3 * na_width), na_bias, l, na_width)

        flat = lambda t: t.reshape(m, -1)
        x2, xb = _merge_call(
            flat(o0), flat(o1), flat(b0), flat(b1), flat(g0), flat(yb), gates, x2, ones_bd,
            row(gn_g[l]), row(gn_b[l]), w_branch_rwkv[l].astype(BF16), w_branch_na[l].astype(BF16),
            w_out[l].astype(BF16), row(ln1_g[l]), row(ln1_b[l]), alpha, tm)

        x2, xb = _ffn_call(
            x2, xb, w_ffn_in[l, :, :d_ff].astype(BF16), w_ffn_in[l, :, d_ff:].astype(BF16),
            w_ffn_out[l].astype(BF16), row(ln2_g[l]), row(ln2_b[l]), alpha, tm)
    return x2.reshape(bsz, s, d)
```
